```python
import math
import jax, jax.numpy as jnp
from jax import lax
import numpy as np

D_MODEL = 1024
BATCH = 8
SEQ = 2048
DEPTH = 4

HEAD_DIM = 64
SB_HEADS = 8
SB_WIDTH = SB_HEADS * HEAD_DIM
NSA_HEADS = 8
NSA_KV_HEADS = 2
NSA_GROUP = NSA_HEADS // NSA_KV_HEADS
NSA_WIDTH = NSA_HEADS * HEAD_DIM
NSA_KV_WIDTH = NSA_KV_HEADS * HEAD_DIM
CMP_LEN = 32
CMP_STRIDE = 16
CMP_HIDDEN = 128
SLC_BLOCK = 64
SLC_TOPK = 8
WINDOW = 256
Q_BLOCK = 128
ROPE_THETA = 10000.0
EPS = 1e-6
NEG = -1e30
FORCE = 1e4
IN_SIZES = (SB_WIDTH, SB_WIDTH, SB_WIDTH, SB_WIDTH,
            NSA_WIDTH, NSA_KV_WIDTH, NSA_KV_WIDTH, NSA_KV_WIDTH, NSA_KV_WIDTH, NSA_KV_WIDTH, NSA_KV_WIDTH,
            NSA_WIDTH, 3 * NSA_HEADS, D_MODEL, D_MODEL)
N_IN = sum(IN_SIZES)

kernel_name = "hybrid_stickbreaking_nsa_adaln"


def rms_norm(x, g):
    xf = x.astype(jnp.float32)
    y = xf * lax.rsqrt(jnp.mean(xf * xf, axis=-1, keepdims=True) + EPS)
    return (y * g.astype(jnp.float32)).astype(x.dtype)


def rope(x, pos):
    half = x.shape[-1] // 2
    freq = ROPE_THETA ** (-jnp.arange(half, dtype=jnp.float32) / half)
    ang = pos.astype(jnp.float32)[:, None] * freq[None, :]
    cos = jnp.cos(ang)[None, :, None, :]
    sin = jnp.sin(ang)[None, :, None, :]
    xf = x.astype(jnp.float32)
    x1, x2 = xf[..., :half], xf[..., half:]
    return jnp.concatenate([x1 * cos - x2 * sin, x2 * cos + x1 * sin], axis=-1).astype(x.dtype)


def stick_breaking_attention(q, k, v):
    B, S, H, d = q.shape
    scale = d ** -0.5
    qh, kh, vh = (a.transpose(0, 2, 1, 3) for a in (q, k, v))
    outs = []
    for i in range(S // Q_BLOCK):
        t0, t1 = i * Q_BLOCK, (i + 1) * Q_BLOCK
        z = jnp.einsum('bhqd,bhkd->bhqk', qh[:, :, t0:t1], kh[:, :, :t1]).astype(jnp.float32) * scale
        strict = jnp.arange(t1)[None, :] < jnp.arange(t0, t1)[:, None]
        log_keep = jnp.where(strict, jax.nn.log_sigmoid(-z), 0.0)
        between = lax.cumsum(log_keep, axis=3, reverse=True) - log_keep
        w = jnp.where(strict, jnp.exp(jax.nn.log_sigmoid(z) + between), 0.0)
        outs.append(jnp.einsum('bhqk,bhkd->bqhd', w.astype(v.dtype), vh[:, :, :t1]))
    return jnp.concatenate(outs, axis=1)


def compress(x, pe, w1, w2):
    B, S, G, d = x.shape
    n = (S - CMP_LEN) // CMP_STRIDE + 1
    idx = CMP_STRIDE * jnp.arange(n)[:, None] + jnp.arange(CMP_LEN)[None, :]
    blocks = x[:, idx] + pe[None, None, :, None, :]
    flat = blocks.transpose(0, 1, 3, 2, 4).reshape(B, n, G, CMP_LEN * d)
    hid = jax.nn.silu(jnp.einsum('bngf,fh->bngh', flat, w1))
    return jnp.einsum('bngh,hd->bngd', hid, w2)


def band_blocks(x, n_q, n_band):
    B, S, G, d = x.shape
    xp = jnp.pad(x.transpose(0, 2, 1, 3), ((0, 0), (0, 0), (WINDOW, 0), (0, 0)))
    xb = xp.reshape(B, G, n_q + n_band - 1, Q_BLOCK, d)
    return jnp.concatenate([xb[:, :, o:o + n_q] for o in range(n_band)], axis=3)


def native_sparse_attention(q, kc, vc, ks, vs, kw, vw, gate_logits):
    B, S, H, d = q.shape
    G = NSA_KV_HEADS
    hpg = NSA_GROUP
    scale = d ** -0.5
    dt = q.dtype
    qg = q.reshape(B, S, G, hpg, d)
    t_all = jnp.arange(S)

    n = kc.shape[1]
    ends = CMP_STRIDE * jnp.arange(n) + CMP_LEN - 1
    valid = ends[None, :] <= t_all[:, None]
    s_c = jnp.einsum('bsghd,bngd->bghsn', qg, kc).astype(jnp.float32) * scale
    p_c = jnp.where(valid, jax.nn.softmax(jnp.where(valid, s_c, NEG), axis=-1), 0.0)
    o_cmp = jnp.einsum('bghsn,bngd->bsghd', p_c.astype(dt), vc)

    nb = S // SLC_BLOCK
    ci = jnp.arange(n)[:, None]
    sj = jnp.arange(nb)[None, :]
    overlap = ((CMP_STRIDE * ci < SLC_BLOCK * (sj + 1)) &
               (CMP_STRIDE * ci + CMP_LEN > SLC_BLOCK * sj)).astype(jnp.float32)
    imp = jnp.einsum('bghsn,nj->bgsj', p_c, overlap)
    cur = (t_all // SLC_BLOCK)[:, None]
    jb = jnp.arange(nb)[None, :]
    imp = jnp.where(jb > cur, NEG, imp)
    imp = jnp.where((jb == 0) | (jb == cur - 1), FORCE, imp)
    imp = jnp.where(jb == cur, 2.0 * FORCE, imp)
    k_sel = min(SLC_TOPK, nb)
    _, sel = lax.top_k(imp, k_sel)

    ks_blocks = ks.transpose(0, 2, 1, 3).reshape(B, G, nb, SLC_BLOCK, d)
    vs_blocks = vs.transpose(0, 2, 1, 3).reshape(B, G, nb, SLC_BLOCK, d)
    n_c = S // Q_BLOCK
    q_chunks = qg.reshape(B, n_c, Q_BLOCK, G, hpg, d).transpose(1, 0, 2, 3, 4, 5)
    sel_chunks = sel.reshape(B, G, n_c, Q_BLOCK, k_sel).transpose(2, 0, 1, 3, 4)
    t_chunks = t_all.reshape(n_c, Q_BLOCK)
    gather = jax.vmap(jax.vmap(lambda blk, ix: blk[ix]))

    def select_chunk(args):
        qc, ic, tc = args
        kg = gather(ks_blocks, ic).reshape(B, G, Q_BLOCK, k_sel * SLC_BLOCK, d)
        vg = gather(vs_blocks, ic).reshape(B, G, Q_BLOCK, k_sel * SLC_BLOCK, d)
        pos = (ic[..., None] * SLC_BLOCK + jnp.arange(SLC_BLOCK)).reshape(B, G, Q_BLOCK, k_sel * SLC_BLOCK)
        mask = (pos <= tc[None, None, :, None])[:, :, None]
        sc = jnp.einsum('bqghd,bgqnd->bghqn', qc, kg).astype(jnp.float32) * scale
        p = jax.nn.softmax(jnp.where(mask, sc, NEG), axis=-1)
        return jnp.einsum('bghqn,bgqnd->bqghd', p.astype(dt), vg)

    o_slc = lax.map(select_chunk, (q_chunks, sel_chunks, t_chunks))
    o_slc = o_slc.transpose(1, 0, 2, 3, 4, 5).reshape(B, S, G, hpg, d)

    n_q = S // Q_BLOCK
    n_band = WINDOW // Q_BLOCK + 1
    kb = band_blocks(kw, n_q, n_band)
    vb = band_blocks(vw, n_q, n_band)
    qb = qg.reshape(B, n_q, Q_BLOCK, G, hpg, d)
    tpos = t_all.reshape(n_q, Q_BLOCK)[:, :, None]
    spos = (jnp.arange(n_q) * Q_BLOCK - WINDOW)[:, None, None] + jnp.arange(n_band * Q_BLOCK)[None, None, :]
    wmask = (spos <= tpos) & (spos > tpos - WINDOW) & (spos >= 0)
    s_w = jnp.einsum('bnqghd,bgnkd->bghnqk', qb, kb).astype(jnp.float32) * scale
    p_w = jax.nn.softmax(jnp.where(wmask, s_w, NEG), axis=-1)
    o_win = jnp.einsum('bghnqk,bgnkd->bnqghd', p_w.astype(dt), vb).reshape(B, S, G, hpg, d)

    g = jax.nn.sigmoid(gate_logits.astype(jnp.float32)).reshape(B, S, 3, G, hpg)[..., None].astype(dt)
    out = g[:, :, 0] * o_cmp + g[:, :, 1] * o_slc + g[:, :, 2] * o_win
    return out.reshape(B, S, H * d)


def hybrid_layer(x, c, ada_w, ada_b, norm_g, w_in, q_norm_g, kc_norm_g, ks_norm_g, kw_norm_g,
                 cmp_pe_k, cmp_w1_k, cmp_w2_k, cmp_pe_v, cmp_w1_v, cmp_w2_v, w_proj_a, w_proj_b, w_out):
    B, S, _ = x.shape
    mod = jax.nn.silu(c) @ ada_w + ada_b
    shift, scale, gate = jnp.split(mod, 3, axis=-1)
    h = rms_norm(x, norm_g) * (1 + scale[:, None]) + shift[:, None]
    proj = h @ w_in
    (sb_q, sb_k, sb_v, sb_z, n_q, k_c, v_c, k_s, v_s, k_w, v_w, n_z, n_g, m_a, m_b) = jnp.split(
        proj, np.cumsum(IN_SIZES)[:-1].tolist(), axis=-1)
    pos = jnp.arange(S)

    def heads(t, nh):
        return t.reshape(B, S, nh, HEAD_DIM)

    a = stick_breaking_attention(heads(sb_q, SB_HEADS), heads(sb_k, SB_HEADS), heads(sb_v, SB_HEADS))
    a = a.reshape(B, S, SB_WIDTH) * jax.nn.silu(sb_z)

    q = rope(rms_norm(heads(n_q, NSA_HEADS), q_norm_g), pos)
    kc = compress(heads(k_c, NSA_KV_HEADS), cmp_pe_k, cmp_w1_k, cmp_w2_k)
    kc = rope(rms_norm(kc, kc_norm_g), CMP_STRIDE * jnp.arange(kc.shape[1]) + CMP_LEN - 1)
    vc = compress(heads(v_c, NSA_KV_HEADS), cmp_pe_v, cmp_w1_v, cmp_w2_v)
    ks = rope(rms_norm(heads(k_s, NSA_KV_HEADS), ks_norm_g), pos)
    kw = rope(rms_norm(heads(k_w, NSA_KV_HEADS), kw_norm_g), pos)
    b = native_sparse_attention(q, kc, vc, ks, heads(v_s, NSA_KV_HEADS), kw, heads(v_w, NSA_KV_HEADS), n_g)
    b = b * jax.nn.silu(n_z)

    y = jax.nn.sigmoid(m_a) * (a @ w_proj_a) + jax.nn.sigmoid(m_b) * (b @ w_proj_b)
    return x + gate[:, None] * (y @ w_out)


def setup_inputs(seed: int = 0) -> dict:
    key = jax.random.key(seed)
    ks = jax.random.split(key, 24)
    L, D, d = DEPTH, D_MODEL, HEAD_DIM
    nrm = lambda k, shape, s: jax.random.normal(k, shape, jnp.float32) * s
    return {
        "x": nrm(ks[0], (BATCH, SEQ, D), 1.0),
        "c": nrm(ks[1], (BATCH, D), 1.0),
        "ada_w": nrm(ks[2], (L, D, 3 * D), 0.5 * D ** -0.5),
        "ada_b": nrm(ks[3], (L, 3 * D), 0.01),
        "norm_g": 1.0 + nrm(ks[4], (L, D), 0.02),
        "w_in": nrm(ks[5], (L, D, N_IN), D ** -0.5),
        "q_norm_g": 1.0 + nrm(ks[6], (L, d), 0.02),
        "kc_norm_g": 1.0 + nrm(ks[7], (L, d), 0.02),
        "ks_norm_g": 1.0 + nrm(ks[8], (L, d), 0.02),
        "kw_norm_g": 1.0 + nrm(ks[9], (L, d), 0.02),
        "cmp_pe_k": nrm(ks[10], (L, CMP_LEN, d), 0.1),
        "cmp_w1_k": nrm(ks[11], (L, CMP_LEN * d, CMP_HIDDEN), (CMP_LEN * d) ** -0.5),
        "cmp_w2_k": nrm(ks[12], (L, CMP_HIDDEN, d), CMP_HIDDEN ** -0.5),
        "cmp_pe_v": nrm(ks[13], (L, CMP_LEN, d), 0.1),
        "cmp_w1_v": nrm(ks[14], (L, CMP_LEN * d, CMP_HIDDEN), (CMP_LEN * d) ** -0.5),
        "cmp_w2_v": nrm(ks[15], (L, CMP_HIDDEN, d), CMP_HIDDEN ** -0.5),
        "w_proj_a": nrm(ks[16], (L, SB_WIDTH, D), SB_WIDTH ** -0.5),
        "w_proj_b": nrm(ks[17], (L, NSA_WIDTH, D), NSA_WIDTH ** -0.5),
        "w_out": nrm(ks[18], (L, D, D), D ** -0.5),
    }


def reference(x, c, ada_w, ada_b, norm_g, w_in, q_norm_g, kc_norm_g, ks_norm_g, kw_norm_g,
              cmp_pe_k, cmp_w1_k, cmp_w2_k, cmp_pe_v, cmp_w1_v, cmp_w2_v, w_proj_a, w_proj_b, w_out):
    for l in range(DEPTH):
        x = hybrid_layer(x, c, ada_w[l], ada_b[l], norm_g[l], w_in[l], q_norm_g[l], kc_norm_g[l],
                         ks_norm_g[l], kw_norm_g[l], cmp_pe_k[l], cmp_w1_k[l], cmp_w2_k[l],
                         cmp_pe_v[l], cmp_w1_v[l], cmp_w2_v[l], w_proj_a[l], w_proj_b[l], w_out[l])
    return x
```

```python
import functools

import jax
import jax.numpy as jnp
import numpy as np
from jax import lax
from jax.experimental import pallas as pl
from jax.experimental.pallas import tpu as pltpu

HEAD_DIM = 64
LANES = 128
SB_HEADS = 8
NSA_HEADS = 8
NSA_KV_HEADS = 2
NSA_GROUP = NSA_HEADS // NSA_KV_HEADS
CMP_LEN = 32
CMP_STRIDE = 16
CMP_HIDDEN = 128
SLC_BLOCK = 64
SLC_TOPK = 8
WINDOW = 256
ROPE_THETA = 10000.0
EPS = 1e-6
NEG = -1e30
FORCE = 1e4
QK_SCALE = HEAD_DIM ** -0.5

F32 = jnp.float32
BF16 = jnp.bfloat16

CB_SBQ, CB_SBK, CB_SBV, CB_SBZ = 0, 4, 8, 12
CB_NQ, CB_NZ, CB_MA, CB_MB = 16, 20, 24, 32
CB_KC, CB_VC, CB_KS, CB_VS, CB_KW, CB_VW, CB_NG = 40, 41, 42, 43, 44, 45, 46
N_PROJ = 48 * LANES

TQ = 128
TK = 128
VMEM_LIMIT = 56 * 1024 * 1024


def _dot(a, b):
    return jnp.dot(a, b, preferred_element_type=F32)


def _dot_nt(a, b):
    return lax.dot_general(a, b, (((1,), (1,)), ((), ())), preferred_element_type=F32)


def _split(x):
    hi = x.astype(BF16)
    lo = (x - hi.astype(F32)).astype(BF16)
    return hi, lo


def _dot_hl(x, m):
    hi, lo = _split(x)
    return _dot(hi, m) + _dot(lo, m)


def _dot3(a, b):
    ah, al = _split(a)
    bh, bl = _split(b)
    return _dot(ah, bh) + _dot(al, bh) + _dot(ah, bl)


def _sigmoid(x):
    return 1.0 / (1.0 + jnp.exp(-x))


def _silu(x):
    return x * _sigmoid(x)


def _iota(shape, dim):
    return lax.broadcasted_iota(jnp.int32, shape, dim)


def _head_block_ones():
    r = _iota((LANES, LANES), 0) // HEAD_DIM
    c = _iota((LANES, LANES), 1) // HEAD_DIM
    return (r == c).astype(BF16)


def _norm_rope(x, g, cos, sin_signed):
    ss = _dot_hl(x * x, _head_block_ones())
    y = x * lax.rsqrt(ss * (1.0 / HEAD_DIM) + EPS) * g
    lane = _iota(y.shape, 1)
    first_half = (lane % HEAD_DIM) < (HEAD_DIM // 2)
    swapped = jnp.where(first_half, pltpu.roll(y, LANES - HEAD_DIM // 2, 1),
                        pltpu.roll(y, HEAD_DIM // 2, 1))
    return y * cos + swapped * sin_signed


def _dup_halves(x):
    lane = _iota(x.shape, 1)
    r = pltpu.roll(x, HEAD_DIM, 1)
    low = lane < HEAD_DIM
    return jnp.where(low, x, r), jnp.where(low, r, x)


def _mod_kernel(c_ref, w_ref, b_ref, o_ref):
    o_ref[...] = _dot3(_silu(c_ref[...]), w_ref[...]) + b_ref[...]


def _modulation(c, ada_w, ada_b):
    L, D, _ = ada_w.shape
    B = c.shape[0]
    return pl.pallas_call(
        _mod_kernel,
        grid=(L, 3),
        in_specs=[
            pl.BlockSpec((B, D), lambda l, j: (0, 0)),
            pl.BlockSpec((None, D, D), lambda l, j: (l, 0, j)),
            pl.BlockSpec((None, 1, D), lambda l, j: (l, 0, j)),
        ],
        out_specs=pl.BlockSpec((None, None, B, D), lambda l, j: (l, j, 0, 0)),
        out_shape=jax.ShapeDtypeStruct((L, 3, B, D), F32),
        compiler_params=pltpu.CompilerParams(dimension_semantics=("arbitrary", "arbitrary")),
        name="adaln_mod",
    )(c, ada_w, ada_b.reshape(L, 1, 3 * D))


def _inproj_kernel(x_ref, shift_ref, scale_ref, g_ref, w_ref, o_ref, h_ref):
    @pl.when(pl.program_id(1) == 0)
    def _():
        x = x_ref[...]
        ms = jnp.mean(x * x, axis=-1, keepdims=True)
        y = x * lax.rsqrt(ms + EPS) * g_ref[...]
        h_ref[...] = (y * (1.0 + scale_ref[...]) + shift_ref[...]).astype(BF16)

    o_ref[...] = _dot(h_ref[...], w_ref[...])


def _in_projection(x2, mod_l, norm_g, w_l, seq):
    M, D = x2.shape
    tm, tn = 512, 1024
    per_b = seq // tm
    B = mod_l.shape[1]
    mod4 = mod_l.reshape(3, B, 1, D)
    return pl.pallas_call(
        _inproj_kernel,
        grid=(M // tm, N_PROJ // tn),
        in_specs=[
            pl.BlockSpec((tm, D), lambda i, j: (i, 0)),
            pl.BlockSpec((None, None, 1, D), lambda i, j: (0, i // per_b, 0, 0)),
            pl.BlockSpec((None, None, 1, D), lambda i, j: (1, i // per_b, 0, 0)),
            pl.BlockSpec((1, D), lambda i, j: (0, 0)),
            pl.BlockSpec((D, tn), lambda i, j: (0, j)),
        ],
        out_specs=pl.BlockSpec((tm, tn), lambda i, j: (i, j)),
        out_shape=jax.ShapeDtypeStruct((M, N_PROJ), F32),
        scratch_shapes=[pltpu.VMEM((tm, D), BF16)],
        compiler_params=pltpu.CompilerParams(
            dimension_semantics=("parallel", "arbitrary"), vmem_limit_bytes=VMEM_LIMIT),
        name="in_projection",
    )(x2, mod4, mod4, norm_g.reshape(1, D), w_l)


def _sb_kernel(q_ref, k_ref, v_ref, z_ref, o_ref):
    i = pl.program_id(2)
    lane = _iota((TQ, LANES), 1)
    low = lane < HEAD_DIM
    q2 = q_ref[...] * QK_SCALE
    qs = jnp.concatenate([jnp.where(low, q2, 0.0), jnp.where(low, 0.0, q2)], axis=0).astype(BF16)

    r = _iota((TK, 2 * TK), 0)
    c = _iota((TK, 2 * TK), 1)
    suffix_ones = ((r > c) | (c >= TK)).astype(BF16)

    def block(kb, acc, carry, diagonal):
        k = k_ref[pl.ds(kb * TK, TK), :].astype(BF16)
        v = v_ref[pl.ds(kb * TK, TK), :].astype(BF16)
        z = _dot_nt(qs, k)
        softplus = jnp.maximum(z, 0.0) + jnp.log(1.0 + jnp.exp(-jnp.abs(z)))
        log_keep = -softplus
        log_hit = z - softplus
        if diagonal:
            row_t = _iota((2 * TQ, TK), 0) % TQ
            strict = _iota((2 * TQ, TK), 1) < row_t
            log_keep = jnp.where(strict, log_keep, 0.0)
        sums = _dot_hl(log_keep, suffix_ones)
        w = jnp.exp(log_hit + sums[:, :TK] + carry)
        if diagonal:
            w = jnp.where(strict, w, 0.0)
        acc = acc + _dot(w.astype(BF16), v)
        return acc, carry + sums[:, TK:]

    zeros = jnp.zeros((2 * TQ, LANES), F32)
    acc, carry = block(i, zeros, zeros, True)

    def body(jj, state):
        return block(i - 1 - jj, state[0], state[1], False)

    acc, carry = lax.fori_loop(0, i, body, (acc, carry))
    out = jnp.where(low, acc[:TQ], acc[TQ:])
    o_ref[...] = (out * _silu(z_ref[...])).astype(BF16)


def _sb_attention(proj3):
    B, S, _ = proj3.shape
    pairs = SB_HEADS // 2
    return pl.pallas_call(
        _sb_kernel,
        grid=(B, pairs, S // TQ),
        in_specs=[
            pl.BlockSpec((None, TQ, LANES), lambda b, p, i: (b, i, CB_SBQ + p)),
            pl.BlockSpec((None, S, LANES), lambda b, p, i: (b, 0, CB_SBK + p)),
            pl.BlockSpec((None, S, LANES), lambda b, p, i: (b, 0, CB_SBV + p)),
            pl.BlockSpec((None, TQ, LANES), lambda b, p, i: (b, i, CB_SBZ + p)),
        ],
        out_specs=pl.BlockSpec((None, TQ, LANES), lambda b, p, i: (b, i, p)),
        out_shape=jax.ShapeDtypeStruct((B, S, pairs * LANES), BF16),
        compiler_params=pltpu.CompilerParams(
            dimension_semantics=("parallel", "parallel", "arbitrary"), vmem_limit_bytes=VMEM_LIMIT),
        name="stick_breaking",
    )(proj3, proj3, proj3, proj3)


def _prep_kernel(q_ref, ks_ref, vs_ref, kw_ref, vw_ref, cos_ref, sin_ref, qg_ref, ksg_ref, kwg_ref,
                 qo_ref, kso_ref, vso_ref, kwo_ref, vwo_ref):
    cos = cos_ref[...]
    sin = sin_ref[...]
    for p in range(NSA_HEADS // 2):
        x = q_ref[:, p * LANES:(p + 1) * LANES]
        qo_ref[:, p * LANES:(p + 1) * LANES] = (
            _norm_rope(x, qg_ref[...], cos, sin) * QK_SCALE).astype(BF16)
    for src, gain, dst in ((ks_ref, ksg_ref, kso_ref), (kw_ref, kwg_ref, kwo_ref)):
        a, b = _dup_halves(_norm_rope(src[...], gain[...], cos, sin))
        dst[0] = a.astype(BF16)
        dst[1] = b.astype(BF16)
    for src, dst in ((vs_ref, vso_ref), (vw_ref, vwo_ref)):
        a, b = _dup_halves(src[...])
        dst[0] = a.astype(BF16)
        dst[1] = b.astype(BF16)


def _nsa_prep(proj3, cos, sin, q_g, ks_g, kw_g):
    B, S, _ = proj3.shape
    tm = 256
    G = NSA_KV_HEADS
    col = lambda cb, w: pl.BlockSpec((None, tm, w * LANES), lambda b, i: (b, i, cb // w))
    tab = pl.BlockSpec((tm, LANES), lambda b, i: (i, 0))
    gain = pl.BlockSpec((1, LANES), lambda b, i: (0, 0))
    kv_out = pl.BlockSpec((None, G, tm, LANES), lambda b, i: (b, 0, i, 0))
    kv_shape = jax.ShapeDtypeStruct((B, G, S, LANES), BF16)
    return pl.pallas_call(
        _prep_kernel,
        grid=(B, S // tm),
        in_specs=[col(CB_NQ, 4), col(CB_KS, 1), col(CB_VS, 1), col(CB_KW, 1), col(CB_VW, 1),
                  tab, tab, gain, gain, gain],
        out_specs=[pl.BlockSpec((None, tm, 4 * LANES), lambda b, i: (b, i, 0)),
                   kv_out, kv_out, kv_out, kv_out],
        out_shape=[jax.ShapeDtypeStruct((B, S, 4 * LANES), BF16), kv_shape, kv_shape, kv_shape, kv_shape],
        compiler_params=pltpu.CompilerParams(
            dimension_semantics=("parallel", "parallel"), vmem_limit_bytes=VMEM_LIMIT),
        name="nsa_prep",
    )(proj3, proj3, proj3, proj3, proj3, cos, sin, q_g, ks_g, kw_g)


def _compress_kernel(xk_ref, xv_ref, pek_ref, pev_ref, w1k_ref, w1v_ref, w2k_ref, w2v_ref,
                     cos_ref, sin_ref, g_ref, ko_ref, vo_ref):
    nrow = xk_ref.shape[0]
    half = CMP_STRIDE * LANES
    hid2 = NSA_KV_HEADS * CMP_HIDDEN

    def mlp(x_ref, pe_ref, w1_ref, w2_ref):
        x = x_ref[...]
        first = _dot((x + pe_ref[0:1, :]).astype(BF16), w1_ref[0:half, :])
        second = _dot((x + pe_ref[1:2, :]).astype(BF16), w1_ref[half:2 * half, :])
        pre = first + pltpu.roll(second, nrow - 1, 0)
        return _dot(_silu(pre).astype(BF16), w2_ref[...])

    kc = _norm_rope(mlp(xk_ref, pek_ref, w1k_ref, w2k_ref), g_ref[...], cos_ref[...], sin_ref[...])
    a, b = _dup_halves(kc)
    ko_ref[0] = a.astype(BF16)
    ko_ref[1] = b.astype(BF16)
    a, b = _dup_halves(mlp(xv_ref, pev_ref, w1v_ref, w2v_ref))
    vo_ref[0] = a.astype(BF16)
    vo_ref[1] = b.astype(BF16)


def _expand_cmp_weights(pe, w1, w2):
    G = NSA_KV_HEADS
    eye = jnp.eye(G, dtype=w1.dtype)
    w1h = w1.reshape(2, CMP_STRIDE, HEAD_DIM, CMP_HIDDEN)
    w1e = jnp.einsum('aldh,gk->algdkh', w1h, eye).reshape(2 * CMP_STRIDE * G * HEAD_DIM, G * CMP_HIDDEN)
    w2e = jnp.einsum('hd,gk->ghkd', w2, eye).reshape(G * CMP_HIDDEN, G * HEAD_DIM)
    pee = jnp.broadcast_to(pe.reshape(2, CMP_STRIDE, 1, HEAD_DIM), (2, CMP_STRIDE, G, HEAD_DIM))
    return pee.reshape(2, CMP_STRIDE * G * HEAD_DIM), w1e.astype(BF16), w2e.astype(BF16)


def _compress(xk, xv, wk, wv, cos_c, sin_c, kc_g):
    B, nrow, width = xk.shape
    G = NSA_KV_HEADS
    full = lambda a: pl.BlockSpec(a.shape, lambda b: (0,) * a.ndim)
    xspec = pl.BlockSpec((None, nrow, width), lambda b: (b, 0, 0))
    ospec = pl.BlockSpec((None, G, nrow, LANES), lambda b: (b, 0, 0, 0))
    oshape = jax.ShapeDtypeStruct((B, G, nrow, LANES), BF16)
    args = (xk, xv, wk[0], wv[0], wk[1], wv[1], wk[2], wv[2], cos_c, sin_c, kc_g)
    return pl.pallas_call(
        _compress_kernel,
        grid=(B,),
        in_specs=[xspec, xspec] + [full(a) for a in args[2:]],
        out_specs=[ospec, ospec],
        out_shape=[oshape, oshape],
        compiler_params=pltpu.CompilerParams(
            dimension_semantics=("parallel",), vmem_limit_bytes=VMEM_LIMIT),
        name="nsa_compress",
    )(*args)


def _nsa_kernel(q_ref, kc_ref, vc_ref, ks_ref, vs_ref, kw_ref, vw_ref, ng_ref, nz_ref, o_ref, *, seq):
    g = pl.program_id(1)
    i = pl.program_id(2)
    R = NSA_GROUP * TQ
    n_blocks = seq // SLC_BLOCK
    n_cmp = (seq - CMP_LEN) // CMP_STRIDE + 1

    lane_q = _iota((TQ, LANES), 1)
    low = lane_q < HEAD_DIM
    q = q_ref[...]
    zero = jnp.zeros((TQ, LANES), BF16)
    slabs = []
    for p in range(NSA_GROUP // 2):
        qp = q[:, p * LANES:(p + 1) * LANES]
        slabs += [jnp.where(low, qp, zero), jnp.where(low, zero, qp)]
    qs = jnp.concatenate(slabs, axis=0)

    t_row = i * TQ + _iota((R, LANES), 0) % TQ
    col = _iota((R, LANES), 1)
    ones_cols = jnp.ones((TK, LANES), BF16)

    nc = kc_ref.shape[0]
    s_c = _dot_nt(qs, kc_ref[...])
    col_c = _iota((R, nc), 1)
    valid = (CMP_STRIDE * col_c + (CMP_LEN - 1) <= i * TQ + _iota((R, nc), 0) % TQ) & (col_c < n_cmp)
    sm = jnp.where(valid, s_c, NEG)
    e = jnp.exp(sm - jnp.max(sm, axis=-1, keepdims=True))
    p_c = jnp.where(valid, e / jnp.sum(e, axis=-1, keepdims=True), 0.0)
    o_cmp = _dot(p_c.astype(BF16), vc_ref[...])

    p_sum = p_c[0:TQ]
    for h in range(1, NSA_GROUP):
        p_sum = p_sum + p_c[h * TQ:(h + 1) * TQ]
    rn = _iota((nc, LANES), 0)
    cj = _iota((nc, LANES), 1)
    overlap = ((CMP_STRIDE * rn < SLC_BLOCK * (cj + 1)) & (CMP_STRIDE * rn + CMP_LEN > SLC_BLOCK * cj)
               & (rn < n_cmp) & (cj < n_blocks)).astype(BF16)
    imp = _dot_hl(p_sum, overlap)
    t_q = i * TQ + _iota((TQ, LANES), 0)
    cur = t_q // SLC_BLOCK
    imp = jnp.where(lane_q > cur, NEG, imp)
    imp = jnp.where((lane_q == 0) | (lane_q == cur - 1), FORCE, imp)
    imp = jnp.where(lane_q == cur, 2.0 * FORCE, imp)
    rank = jnp.zeros((TQ, LANES), jnp.int32)
    for j in range(n_blocks):
        cj_val = jnp.broadcast_to(imp[:, j:j + 1], (TQ, LANES))
        ahead = (cj_val > imp) | ((cj_val == imp) & (lane_q > j))
        rank = rank + ahead.astype(jnp.int32)
    chosen = ((rank < min(SLC_TOPK, n_blocks)) & (lane_q < n_blocks)).astype(BF16)
    chosen4 = jnp.concatenate([chosen] * NSA_GROUP, axis=0)

    def slc_step(c, state):
        m, acc = state
        k = ks_ref[pl.ds(c * TK, TK), :]
        v = vs_ref[pl.ds(c * TK, TK), :]
        s = _dot_nt(qs, k)
        key_block = (c * TK + _iota((LANES, TK), 1)) // SLC_BLOCK
        expand = (_iota((LANES, TK), 0) == key_block).astype(BF16)
        picked = _dot(chosen4, expand)
        kpos = c * TK + col
        allowed = (picked > 0.5) & (kpos <= t_row)
        sm = jnp.where(allowed, s, NEG)
        m_new = jnp.maximum(m, jnp.max(sm, axis=-1, keepdims=True))
        p = jnp.where(allowed, jnp.exp(sm - m_new), 0.0).astype(BF16)
        pv = jnp.concatenate([_dot(p, v), _dot(p, ones_cols)], axis=1)
        return m_new, jnp.exp(m - m_new) * acc + pv

    m0 = jnp.full((R, 1), NEG, F32)
    acc0 = jnp.zeros((R, 2 * LANES), F32)
    _, acc = lax.fori_loop(0, i + 1, slc_step, (m0, acc0))
    o_slc = acc[:, :LANES] / acc[:, LANES:]

    span = WINDOW + TQ
    start = jnp.clip(i * TQ - WINDOW, 0, seq - span)
    start = pl.multiple_of(start, TQ)
    kw = kw_ref[pl.ds(start, span), :]
    vw = vw_ref[pl.ds(start, span), :]
    s_w = _dot_nt(qs, kw)
    kpos = start + _iota((R, span), 1)
    t_w = i * TQ + _iota((R, span), 0) % TQ
    allowed = (kpos <= t_w) & (kpos > t_w - WINDOW)
    sm = jnp.where(allowed, s_w, NEG)
    e = jnp.exp(sm - jnp.max(sm, axis=-1, keepdims=True)).astype(BF16)
    o_win = _dot(e, vw) / _dot(e, jnp.ones((span, LANES), BF16))

    sig = _sigmoid(ng_ref[...])
    sig_hi, sig_lo = _split(sig)
    outs = []
    for h in range(NSA_GROUP):
        rows = slice(h * TQ, (h + 1) * TQ)
        total = None
        for br, o in enumerate((o_cmp, o_slc, o_win)):
            pick = (_iota((LANES, LANES), 0) == br * NSA_HEADS + g * NSA_GROUP + h).astype(BF16)
            gate = _dot(sig_hi, pick) + _dot(sig_lo, pick)
            term = gate * o[rows]
            total = term if total is None else total + term
        outs.append(total)
    for p in range(NSA_GROUP // 2):
        pair = jnp.where(low, outs[2 * p], outs[2 * p + 1])
        zp = nz_ref[:, p * LANES:(p + 1) * LANES]
        o_ref[:, p * LANES:(p + 1) * LANES] = (pair * _silu(zp)).astype(BF16)


def _nsa_core(qn, kc, vc, ks, vs, kw, vw, proj3):
    B, S, _ = proj3.shape
    G = NSA_KV_HEADS
    gw = NSA_GROUP // 2
    ncmp = kc.shape[2]
    kvc = pl.BlockSpec((None, None, ncmp, LANES), lambda b, g, i: (b, g, 0, 0))
    kvs = pl.BlockSpec((None, None, S, LANES), lambda b, g, i: (b, g, 0, 0))
    return pl.pallas_call(
        functools.partial(_nsa_kernel, seq=S),
        grid=(B, G, S // TQ),
        in_specs=[
            pl.BlockSpec((None, TQ, gw * LANES), lambda b, g, i: (b, i, g)),
            kvc, kvc, kvs, kvs, kvs, kvs,
            pl.BlockSpec((None, TQ, LANES), lambda b, g, i: (b, i, CB_NG)),
            pl.BlockSpec((None, TQ, gw * LANES), lambda b, g, i: (b, i, CB_NZ // gw + g)),
        ],
        out_specs=pl.BlockSpec((None, TQ, gw * LANES), lambda b, g, i: (b, i, g)),
        out_shape=jax.ShapeDtypeStruct((B, S, NSA_HEADS * HEAD_DIM), BF16),
        compiler_params=pltpu.CompilerParams(
            dimension_semantics=("parallel", "parallel", "arbitrary"), vmem_limit_bytes=VMEM_LIMIT),
        name="nsa_core",
    )(qn, kc, vc, ks, vs, kw, vw, proj3, proj3)


def _merge_kernel(a_ref, b_ref, ma_ref, mb_ref, x_ref, gate_ref, wa_ref, wb_ref, wo_ref, o_ref):
    ya = _dot(a_ref[...], wa_ref[...])
    yb = _dot(b_ref[...], wb_ref[...])
    y = _sigmoid(ma_ref[...]) * ya + _sigmoid(mb_ref[...]) * yb
    o_ref[...] = x_ref[...] + gate_ref[...] * _dot(y.astype(BF16), wo_ref[...])


def _merge(a2, b2, proj2, x2, mod_l, wa, wb, wo, seq):
    M, D = x2.shape
    tm = 256
    per_b = seq // tm
    B = mod_l.shape[1]
    mod4 = mod_l.reshape(3, B, 1, D)
    dcols = D // LANES
    full = lambda a: pl.BlockSpec(a.shape, lambda i: (0, 0))
    return pl.pallas_call(
        _merge_kernel,
        grid=(M // tm,),
        in_specs=[
            pl.BlockSpec((tm, a2.shape[1]), lambda i: (i, 0)),
            pl.BlockSpec((tm, b2.shape[1]), lambda i: (i, 0)),
            pl.BlockSpec((tm, D), lambda i: (i, CB_MA // dcols)),
            pl.BlockSpec((tm, D), lambda i: (i, CB_MB // dcols)),
            pl.BlockSpec((tm, D), lambda i: (i, 0)),
            pl.BlockSpec((None, None, 1, D), lambda i: (2, i // per_b, 0, 0)),
            full(wa), full(wb), full(wo),
        ],
        out_specs=pl.BlockSpec((tm, D), lambda i: (i, 0)),
        out_shape=jax.ShapeDtypeStruct((M, D), F32),
        compiler_params=pltpu.CompilerParams(
            dimension_semantics=("parallel",), vmem_limit_bytes=VMEM_LIMIT),
        name="merge_out",
    )(a2, b2, proj2, proj2, x2, mod4, wa, wb, wo)


def _permute_w_in(w_in):
    sb, nw, kv = SB_HEADS * HEAD_DIM, NSA_HEADS * HEAD_DIM, NSA_KV_HEADS * HEAD_DIM
    D = w_in.shape[1]
    sizes = (sb, sb, sb, sb, nw, kv, kv, kv, kv, kv, kv, nw, 3 * NSA_HEADS, D, D)
    offs = np.concatenate([[0], np.cumsum(sizes)])
    assert offs[-1] == w_in.shape[2]
    seg = lambda k: w_in[:, :, offs[k]:offs[k + 1]]
    pad = lambda n: jnp.zeros(w_in.shape[:2] + (n,), w_in.dtype)
    parts = [seg(0), seg(1), seg(2), seg(3), seg(4), seg(11), seg(13), seg(14),
             seg(5), seg(6), seg(7), seg(8), seg(9), seg(10), seg(12), pad(LANES - 3 * NSA_HEADS), pad(LANES)]
    out = jnp.concatenate(parts, axis=-1)
    assert out.shape[-1] == N_PROJ
    return out.astype(BF16)


def _rope_tables(pos):
    half = HEAD_DIM // 2
    freq = ROPE_THETA ** (-jnp.arange(half, dtype=F32) / half)
    ang = pos.astype(F32)[:, None] * freq[None, :]
    cos, sin = jnp.cos(ang), jnp.sin(ang)
    reps = LANES // HEAD_DIM
    return jnp.tile(jnp.concatenate([cos, cos], -1), (1, reps)), jnp.tile(jnp.concatenate([-sin, sin], -1), (1, reps))


def kernel(x, c, ada_w, ada_b, norm_g, w_in, q_norm_g, kc_norm_g, ks_norm_g, kw_norm_g, cmp_pe_k, cmp_w1_k, cmp_w2_k, cmp_pe_v, cmp_w1_v, cmp_w2_v, w_proj_a, w_proj_b, w_out):
    B, S, D = x.shape
    L = ada_w.shape[0]
    M = B * S
    assert D == 8 * LANES and S % 512 == 0 and S // SLC_BLOCK >= SLC_TOPK and S >= WINDOW + TQ
    n_cmp_rows = S // CMP_STRIDE

    mod = _modulation(c, ada_w, ada_b)
    w_in_p = _permute_w_in(w_in)
    wa, wb, wo = w_proj_a.astype(BF16), w_proj_b.astype(BF16), w_out.astype(BF16)
    cos, sin = _rope_tables(jnp.arange(S))
    cos_c, sin_c = _rope_tables(CMP_STRIDE * jnp.arange(n_cmp_rows) + CMP_LEN - 1)
    tile2 = lambda g: jnp.tile(g, (1, LANES // HEAD_DIM)).reshape(L, 1, LANES)
    q_g, kc_g, ks_g, kw_g = tile2(q_norm_g), tile2(kc_norm_g), tile2(ks_norm_g), tile2(kw_norm_g)

    x2 = x.reshape(M, D)
    for l in range(L):
        proj2 = _in_projection(x2, mod[l], norm_g[l], w_in_p[l], S)
        proj3 = proj2.reshape(B, S, N_PROJ)
        a = _sb_attention(proj3)
        qn, ks, vs, kw, vw = _nsa_prep(proj3, cos, sin, q_g[l], ks_g[l], kw_g[l])
        xk = proj3[:, :, CB_KC * LANES:(CB_KC + 1) * LANES].reshape(B, n_cmp_rows, CMP_STRIDE * LANES)
        xv = proj3[:, :, CB_VC * LANES:(CB_VC + 1) * LANES].reshape(B, n_cmp_rows, CMP_STRIDE * LANES)
        kc, vc = _compress(xk, xv,
                           _expand_cmp_weights(cmp_pe_k[l], cmp_w1_k[l], cmp_w2_k[l]),
                           _expand_cmp_weights(cmp_pe_v[l], cmp_w1_v[l], cmp_w2_v[l]),
                           cos_c, sin_c, kc_g[l])
        b = _nsa_core(qn, kc, vc, ks, vs, kw, vw, proj3)
        x2 = _merge(a.reshape(M, -1), b.reshape(M, -1), proj2, x2, mod[l], wa[l], wb[l], wo[l], S)
    return x2.reshape(B, S, D)
```

```python
import functools
import math

import jax
import jax.numpy as jnp
import numpy as np
from jax import lax
from jax.experimental import pallas as pl
from jax.experimental.pallas import tpu as pltpu

HEAD_DIM = 64
LANES = 128
SB_HEADS = 8
NSA_HEADS = 8
NSA_KV_HEADS = 2
NSA_GROUP = NSA_HEADS // NSA_KV_HEADS
CMP_LEN = 32
CMP_STRIDE = 16
CMP_HIDDEN = 128
SLC_BLOCK = 64
SLC_TOPK = 8
WINDOW = 256
ROPE_THETA = 10000.0
EPS = 1e-6
NEG = -1e30
FORCE = 1e4
Q_SCALE2 = HEAD_DIM ** -0.5 * math.log2(math.e)
BIG = 2.0 ** 100

F32 = jnp.float32
BF16 = jnp.bfloat16

CB_SBQ, CB_SBK, CB_SBV, CB_SBZ = 0, 4, 8, 12
N_PROJ_A = 16 * LANES
CB_NQ, CB_NZ, CB_MA, CB_MB = 0, 4, 8, 16
CB_KC, CB_VC, CB_KS, CB_VS, CB_KW, CB_VW, CB_NG = 24, 25, 26, 27, 28, 29, 30
N_PROJ_B = 32 * LANES

TQ = 128
SLC_TK = 256
SB_T = 256
VMEM_LIMIT = 56 * 1024 * 1024


def _dot(a, b):
    return jnp.dot(a, b, preferred_element_type=F32)


def _dot_nt(a, b):
    return lax.dot_general(a, b, (((1,), (1,)), ((), ())), preferred_element_type=F32)


def _split(x):
    hi = x.astype(BF16)
    lo = (x - hi.astype(F32)).astype(BF16)
    return hi, lo


def _dot_hl(x, m):
    hi, lo = _split(x)
    return _dot(hi, m) + _dot(lo, m)


def _dot3(a, b):
    ah, al = _split(a)
    bh, bl = _split(b)
    return _dot(ah, bh) + _dot(al, bh) + _dot(ah, bl)


def _sigmoid(x):
    return 1.0 / (1.0 + jnp.exp(-x))


def _silu(x):
    return x * _sigmoid(x)


def _iota(shape, dim):
    return lax.broadcasted_iota(jnp.int32, shape, dim)


def _head_block_ones():
    r = _iota((LANES, LANES), 0) // HEAD_DIM
    c = _iota((LANES, LANES), 1) // HEAD_DIM
    return (r == c).astype(BF16)


def _norm_rope(x, g, cos, sin_signed):
    ss = _dot_hl(x * x, _head_block_ones())
    y = x * lax.rsqrt(ss * (1.0 / HEAD_DIM) + EPS) * g
    lane = _iota(y.shape, 1)
    first_half = (lane % HEAD_DIM) < (HEAD_DIM // 2)
    swapped = jnp.where(first_half, pltpu.roll(y, LANES - HEAD_DIM // 2, 1),
                        pltpu.roll(y, HEAD_DIM // 2, 1))
    return y * cos + swapped * sin_signed


def _dup_halves(x):
    lane = _iota(x.shape, 1)
    r = pltpu.roll(x, HEAD_DIM, 1)
    low = lane < HEAD_DIM
    return jnp.where(low, x, r), jnp.where(low, r, x)


def _mod_kernel(c_ref, w_ref, b_ref, o_ref):
    o_ref[...] = _dot3(_silu(c_ref[...]), w_ref[...]) + b_ref[...]


def _modulation(c, ada_w, ada_b):
    L, D, _ = ada_w.shape
    B = c.shape[0]
    return pl.pallas_call(
        _mod_kernel,
        grid=(L, 3),
        in_specs=[
            pl.BlockSpec((B, D), lambda l, j: (0, 0)),
            pl.BlockSpec((None, D, D), lambda l, j: (l, 0, j)),
            pl.BlockSpec((None, 1, D), lambda l, j: (l, 0, j)),
        ],
        out_specs=pl.BlockSpec((None, None, B, D), lambda l, j: (l, j, 0, 0)),
        out_shape=jax.ShapeDtypeStruct((L, 3, B, D), F32),
        compiler_params=pltpu.CompilerParams(dimension_semantics=("arbitrary", "arbitrary")),
        name="adaln_mod",
    )(c, ada_w, ada_b.reshape(L, 1, 3 * D))


def _inproj_kernel(x_ref, shift_ref, scale_ref, g_ref, w_ref, o_ref, h_ref):
    @pl.when(pl.program_id(1) == 0)
    def _():
        x = x_ref[...]
        ms = jnp.mean(x * x, axis=-1, keepdims=True)
        y = x * lax.rsqrt(ms + EPS) * g_ref[...]
        h_ref[...] = (y * (1.0 + scale_ref[...]) + shift_ref[...]).astype(BF16)

    o_ref[...] = _dot(h_ref[...], w_ref[...]).astype(o_ref.dtype)


def _in_projection(x2, mod_l, norm_g, w_l, seq, out_dtype):
    M, D = x2.shape
    N = w_l.shape[1]
    tm, tn = 512, 1024
    per_b = seq // tm
    B = mod_l.shape[1]
    mod4 = mod_l.reshape(3, B, 1, D)
    return pl.pallas_call(
        _inproj_kernel,
        grid=(M // tm, N // tn),
        in_specs=[
            pl.BlockSpec((tm, D), lambda i, j: (i, 0)),
            pl.BlockSpec((None, None, 1, D), lambda i, j: (0, i // per_b, 0, 0)),
            pl.BlockSpec((None, None, 1, D), lambda i, j: (1, i // per_b, 0, 0)),
            pl.BlockSpec((1, D), lambda i, j: (0, 0)),
            pl.BlockSpec((D, tn), lambda i, j: (0, j)),
        ],
        out_specs=pl.BlockSpec((tm, tn), lambda i, j: (i, j)),
        out_shape=jax.ShapeDtypeStruct((M, N), out_dtype),
        scratch_shapes=[pltpu.VMEM((tm, D), BF16)],
        compiler_params=pltpu.CompilerParams(
            dimension_semantics=("parallel", "arbitrary"), vmem_limit_bytes=VMEM_LIMIT),
        name="in_projection",
    )(x2, mod4, mod4, norm_g.reshape(1, D), w_l)


def _sb_kernel(q_ref, k_ref, v_ref, z_ref, o_ref):
    T = SB_T
    i = pl.program_id(2)
    lane = _iota((T, LANES), 1)
    low = lane < HEAD_DIM
    q2 = q_ref[...]
    zero = jnp.zeros_like(q2)
    q_heads = (jnp.where(low, q2, zero), jnp.where(low, zero, q2))

    r = _iota((T, T + LANES), 0)
    c = _iota((T, T + LANES), 1)
    suffix_ones = ((r > c) | (c >= T)).astype(BF16)
    strict = _iota((T, T), 1) < _iota((T, T), 0)

    def chain(qh, k, v, acc, carry, diagonal):
        z = _dot_nt(qh, k)
        log_keep = jnp.minimum(-z, 0.0) - jnp.log2(1.0 + jnp.exp2(-jnp.abs(z)))
        log_hit = z + log_keep
        if diagonal:
            log_keep = jnp.where(strict, log_keep, 0.0)
        sums = _dot_hl(log_keep, suffix_ones)
        carry2 = jnp.concatenate([carry] * (T // LANES), axis=1)
        w = jnp.exp2(log_hit + sums[:, :T] + carry2)
        if diagonal:
            w = jnp.where(strict, w, 0.0)
        return acc + _dot(w.astype(BF16), v), carry + sums[:, T:]

    def block(kb, state, diagonal):
        k = k_ref[pl.ds(pl.multiple_of(kb * T, T), T), :]
        v = v_ref[pl.ds(pl.multiple_of(kb * T, T), T), :]
        out = []
        for h in range(2):
            out += chain(q_heads[h], k, v, state[2 * h], state[2 * h + 1], diagonal)
        return tuple(out)

    zeros = jnp.zeros((T, LANES), F32)
    state = block(i, (zeros,) * 4, True)
    state = lax.fori_loop(0, i, lambda jj, st: block(i - 1 - jj, st, False), state)
    out = jnp.where(low, state[0], state[2])
    o_ref[...] = (out * _silu(z_ref[...].astype(F32))).astype(BF16)


def _sb_attention(proj_a):
    B, S, _ = proj_a.shape
    pairs = SB_HEADS // 2
    T = SB_T
    return pl.pallas_call(
        _sb_kernel,
        grid=(B, pairs, S // T),
        in_specs=[
            pl.BlockSpec((None, T, LANES), lambda b, p, i: (b, i, CB_SBQ + p)),
            pl.BlockSpec((None, S, LANES), lambda b, p, i: (b, 0, CB_SBK + p)),
            pl.BlockSpec((None, S, LANES), lambda b, p, i: (b, 0, CB_SBV + p)),
            pl.BlockSpec((None, T, LANES), lambda b, p, i: (b, i, CB_SBZ + p)),
        ],
        out_specs=pl.BlockSpec((None, T, LANES), lambda b, p, i: (b, i, p)),
        out_shape=jax.ShapeDtypeStruct((B, S, pairs * LANES), BF16),
        compiler_params=pltpu.CompilerParams(
            dimension_semantics=("parallel", "parallel", "arbitrary"), vmem_limit_bytes=VMEM_LIMIT),
        name="stick_breaking",
    )(proj_a, proj_a, proj_a, proj_a)


def _prep_kernel(q_ref, ks_ref, vs_ref, kw_ref, vw_ref, cos_ref, sin_ref, qg_ref, ksg_ref, kwg_ref,
                 qo_ref, kso_ref, vso_ref, kwo_ref, vwo_ref):
    tm = q_ref.shape[0]
    cos = cos_ref[...]
    sin = sin_ref[...]
    for p in range(NSA_HEADS // 2):
        x = q_ref[:, p * LANES:(p + 1) * LANES]
        qo_ref[:, p * LANES:(p + 1) * LANES] = (
            _norm_rope(x, qg_ref[...], cos, sin) * Q_SCALE2).astype(BF16)
    pos = pl.program_id(1) * tm + _iota((tm, LANES), 0)
    own_block = jnp.where(_iota((tm, LANES), 1) == pos // SLC_BLOCK, -BIG, 0.0).astype(BF16)
    ones = jnp.ones((tm, LANES), BF16)
    for g, half in enumerate(_dup_halves(_norm_rope(ks_ref[...], ksg_ref[...], cos, sin))):
        kso_ref[g] = jnp.concatenate([half.astype(BF16), own_block], axis=1)
    for g, half in enumerate(_dup_halves(_norm_rope(kw_ref[...], kwg_ref[...], cos, sin))):
        kwo_ref[g] = half.astype(BF16)
    for src, dst in ((vs_ref, vso_ref), (vw_ref, vwo_ref)):
        for g, half in enumerate(_dup_halves(src[...])):
            dst[g] = jnp.concatenate([half.astype(BF16), ones], axis=1)


def _nsa_prep(proj_b, cos, sin, q_g, ks_g, kw_g):
    B, S, _ = proj_b.shape
    tm = 256
    G = NSA_KV_HEADS
    col = lambda cb, w: pl.BlockSpec((None, tm, w * LANES), lambda b, i: (b, i, cb // w))
    tab = pl.BlockSpec((tm, LANES), lambda b, i: (i, 0))
    gain = pl.BlockSpec((1, LANES), lambda b, i: (0, 0))
    kv_out = lambda w: pl.BlockSpec((None, G, tm, w * LANES), lambda b, i: (b, 0, i, 0))
    kv_shape = lambda w: jax.ShapeDtypeStruct((B, G, S, w * LANES), BF16)
    return pl.pallas_call(
        _prep_kernel,
        grid=(B, S // tm),
        in_specs=[col(CB_NQ, 4), col(CB_KS, 1), col(CB_VS, 1), col(CB_KW, 1), col(CB_VW, 1),
                  tab, tab, gain, gain, gain],
        out_specs=[pl.BlockSpec((None, tm, 4 * LANES), lambda b, i: (b, i, 0)),
                   kv_out(2), kv_out(2), kv_out(1), kv_out(2)],
        out_shape=[jax.ShapeDtypeStruct((B, S, 4 * LANES), BF16),
                   kv_shape(2), kv_shape(2), kv_shape(1), kv_shape(2)],
        compiler_params=pltpu.CompilerParams(
            dimension_semantics=("parallel", "parallel"), vmem_limit_bytes=VMEM_LIMIT),
        name="nsa_prep",
    )(proj_b, proj_b, proj_b, proj_b, proj_b, cos, sin, q_g, ks_g, kw_g)


def _compress_kernel(xk_ref, xv_ref, pek_ref, pev_ref, w1k_ref, w1v_ref, w2k_ref, w2v_ref,
                     cos_ref, sin_ref, g_ref, ko_ref, vo_ref):
    nrow = xk_ref.shape[0]
    half = CMP_STRIDE * LANES

    def mlp(x_ref, pe_ref, w1_ref, w2_ref):
        x = x_ref[...]
        first = _dot((x + pe_ref[0:1, :]).astype(BF16), w1_ref[0:half, :])
        second = _dot((x + pe_ref[1:2, :]).astype(BF16), w1_ref[half:2 * half, :])
        pre = first + pltpu.roll(second, nrow - 1, 0)
        return _dot(_silu(pre).astype(BF16), w2_ref[...])

    kc = _norm_rope(mlp(xk_ref, pek_ref, w1k_ref, w2k_ref), g_ref[...], cos_ref[...], sin_ref[...])
    a, b = _dup_halves(kc)
    ko_ref[0] = a.astype(BF16)
    ko_ref[1] = b.astype(BF16)
    a, b = _dup_halves(mlp(xv_ref, pev_ref, w1v_ref, w2v_ref))
    vo_ref[0] = a.astype(BF16)
    vo_ref[1] = b.astype(BF16)


def _expand_cmp_weights(pe, w1, w2):
    G = NSA_KV_HEADS
    eye = jnp.eye(G, dtype=w1.dtype)
    w1h = w1.reshape(2, CMP_STRIDE, HEAD_DIM, CMP_HIDDEN)
    w1e = jnp.einsum('aldh,gk->algdkh', w1h, eye).reshape(2 * CMP_STRIDE * G * HEAD_DIM, G * CMP_HIDDEN)
    w2e = jnp.einsum('hd,gk->ghkd', w2, eye).reshape(G * CMP_HIDDEN, G * HEAD_DIM)
    pee = jnp.broadcast_to(pe.reshape(2, CMP_STRIDE, 1, HEAD_DIM), (2, CMP_STRIDE, G, HEAD_DIM))
    return pee.reshape(2, CMP_STRIDE * G * HEAD_DIM), w1e.astype(BF16), w2e.astype(BF16)


def _compress(xk, xv, wk, wv, cos_c, sin_c, kc_g):
    B, nrow, width = xk.shape
    G = NSA_KV_HEADS
    full = lambda a: pl.BlockSpec(a.shape, lambda b: (0,) * a.ndim)
    xspec = pl.BlockSpec((None, nrow, width), lambda b: (b, 0, 0))
    ospec = pl.BlockSpec((None, G, nrow, LANES), lambda b: (b, 0, 0, 0))
    oshape = jax.ShapeDtypeStruct((B, G, nrow, LANES), BF16)
    args = (xk, xv, wk[0], wv[0], wk[1], wv[1], wk[2], wv[2], cos_c, sin_c, kc_g)
    return pl.pallas_call(
        _compress_kernel,
        grid=(B,),
        in_specs=[xspec, xspec] + [full(a) for a in args[2:]],
        out_specs=[ospec, ospec],
        out_shape=[oshape, oshape],
        compiler_params=pltpu.CompilerParams(
            dimension_semantics=("parallel",), vmem_limit_bytes=VMEM_LIMIT),
        name="nsa_compress",
    )(*args)


def _nsa_kernel(q_ref, kc_ref, vc_ref, ks_ref, vs_ref, kw_ref, vw_ref, ng_ref, nz_ref, o_ref, s_ref, *, seq):
    g = pl.program_id(1)
    i = pl.program_id(2)
    R = NSA_GROUP * TQ
    n_blocks = seq // SLC_BLOCK
    n_cmp = (seq - CMP_LEN) // CMP_STRIDE + 1
    nc = kc_ref.shape[0]
    t0 = i * TQ

    lane_q = _iota((TQ, LANES), 1)
    low = lane_q < HEAD_DIM
    q = q_ref[...]
    zero = jnp.zeros((TQ, LANES), BF16)
    slabs = []
    for p in range(NSA_GROUP // 2):
        qp = q[:, p * LANES:(p + 1) * LANES]
        slabs += [jnp.where(low, qp, zero), jnp.where(low, zero, qp)]
    qs = jnp.concatenate(slabs, axis=0)
    onehot_t = (_iota((R, LANES), 1) == _iota((R, LANES), 0) % TQ).astype(BF16)
    qs_pos = jnp.concatenate([qs, onehot_t], axis=1)

    def masked_bias(allowed):
        return jnp.where(allowed, 0.0, -BIG).astype(BF16)

    n_i = _iota((nc, LANES), 0)
    t_i = t0 + _iota((nc, LANES), 1)
    bias_c = masked_bias((CMP_STRIDE * n_i + (CMP_LEN - 1) <= t_i) & (n_i < n_cmp))
    s_c = _dot_nt(qs_pos, jnp.concatenate([kc_ref[...], bias_c], axis=1))
    e = jnp.exp2(s_c - jnp.max(s_c, axis=-1, keepdims=True))
    p_c = e / jnp.sum(e, axis=-1, keepdims=True)
    has_valid = t0 + _iota((R, nc), 0) % TQ >= CMP_LEN - 1
    p_c = jnp.where(has_valid, p_c, 0.0)
    o_cmp = _dot(p_c.astype(BF16), vc_ref[...])

    p_sum = p_c[0:TQ]
    for h in range(1, NSA_GROUP):
        p_sum = p_sum + p_c[h * TQ:(h + 1) * TQ]
    jn = _iota((LANES, nc), 0)
    nn = _iota((LANES, nc), 1)
    overlap_t = ((CMP_STRIDE * nn < SLC_BLOCK * (jn + 1)) & (CMP_STRIDE * nn + CMP_LEN > SLC_BLOCK * jn)
                 & (nn < n_cmp) & (jn < n_blocks)).astype(BF16)
    p_hi, p_lo = _split(p_sum)
    imp_t = _dot_nt(overlap_t, p_hi) + _dot_nt(overlap_t, p_lo)
    nb8 = -(-n_blocks // 8) * 8
    imp = imp_t[:nb8]
    jb = _iota((nb8, TQ), 0)
    cur = (t0 + _iota((nb8, TQ), 1)) // SLC_BLOCK
    imp = jnp.where(jb > cur, NEG, imp)
    imp = jnp.where((jb == 0) | (jb == cur - 1), FORCE, imp)
    imp = jnp.where(jb == cur, 2.0 * FORCE, imp)
    rank = jnp.zeros((nb8, TQ), jnp.int32)
    for j in range(n_blocks):
        row = jnp.broadcast_to(imp[j:j + 1, :], (nb8, TQ))
        ahead = (row > imp) | ((row == imp) & (jb > j))
        rank = rank + ahead.astype(jnp.int32)
    chosen_t = (rank < min(SLC_TOPK, n_blocks)) & (jb <= cur) & (jb < n_blocks)
    not_t = jnp.where(chosen_t, 0.0, 1.0)
    if nb8 < LANES:
        not_t = jnp.concatenate([not_t, jnp.zeros((LANES - nb8, TQ), F32)], axis=0)
    eye = (_iota((TQ, TQ), 0) == _iota((TQ, TQ), 1)).astype(BF16)
    not_chosen = _dot_nt(eye, not_t.astype(BF16)).astype(BF16)
    qs_sel = jnp.concatenate([qs, jnp.concatenate([not_chosen] * NSA_GROUP, axis=0)], axis=1)

    last = (t0 + TQ - 1) // SLC_TK

    def chunk(ref, c):
        return ref[pl.ds(pl.multiple_of(c * SLC_TK, SLC_TK), SLC_TK), :]

    def pass1(c, run):
        s = _dot_nt(qs_sel, chunk(ks_ref, c))
        s_ref[c] = s
        return jnp.maximum(run, s)

    run = lax.fori_loop(0, last, pass1, jnp.full((R, SLC_TK), -3.0e38, F32))
    s = _dot_nt(qs_sel, chunk(ks_ref, last))
    causal = last * SLC_TK + _iota((R, SLC_TK), 1) <= t0 + _iota((R, SLC_TK), 0) % TQ
    s = jnp.where(causal, s, -BIG)
    s_ref[last] = s
    run = jnp.maximum(run, s)
    m_sel = jnp.broadcast_to(jnp.max(run, axis=-1, keepdims=True), (R, SLC_TK))

    def pass2(c, acc):
        p = jnp.exp2(s_ref[c] - m_sel).astype(BF16)
        return acc + _dot(p, chunk(vs_ref, c))

    acc = lax.fori_loop(0, last + 1, pass2, jnp.zeros((R, 2 * LANES), F32))
    o_slc = acc[:, :LANES] / acc[:, LANES:]

    span = WINDOW + TQ
    start = pl.multiple_of(jnp.clip(t0 - WINDOW, 0, seq - span), TQ)
    k_pos = start + _iota((span, LANES), 0)
    t_w = t0 + _iota((span, LANES), 1)
    bias_w = masked_bias((k_pos <= t_w) & (k_pos > t_w - WINDOW))
    s_w = _dot_nt(qs_pos, jnp.concatenate([kw_ref[pl.ds(start, span), :], bias_w], axis=1))
    e = jnp.exp2(s_w - jnp.max(s_w, axis=-1, keepdims=True)).astype(BF16)
    ow = _dot(e, vw_ref[pl.ds(start, span), :])
    o_win = ow[:, :LANES] / ow[:, LANES:]

    sig_hi, sig_lo = _split(_sigmoid(ng_ref[...]))
    outs = []
    for h in range(NSA_GROUP):
        rows = slice(h * TQ, (h + 1) * TQ)
        total = None
        for br, o in enumerate((o_cmp, o_slc, o_win)):
            pick = (_iota((LANES, LANES), 0) == br * NSA_HEADS + g * NSA_GROUP + h).astype(BF16)
            gate = _dot(sig_hi, pick) + _dot(sig_lo, pick)
            term = gate * o[rows]
            total = term if total is None else total + term
        outs.append(total)
    for p in range(NSA_GROUP // 2):
        pair = jnp.where(low, outs[2 * p], outs[2 * p + 1])
        zp = nz_ref[:, p * LANES:(p + 1) * LANES]
        o_ref[:, p * LANES:(p + 1) * LANES] = (pair * _silu(zp)).astype(BF16)


def _nsa_core(qn, kc, vc, ks, vs, kw, vw, proj_b):
    B, S, _ = proj_b.shape
    G = NSA_KV_HEADS
    gw = NSA_GROUP // 2
    ncmp = kc.shape[2]
    kvc = pl.BlockSpec((None, None, ncmp, LANES), lambda b, g, i: (b, g, 0, 0))
    kvs = lambda w: pl.BlockSpec((None, None, S, w * LANES), lambda b, g, i: (b, g, 0, 0))
    return pl.pallas_call(
        functools.partial(_nsa_kernel, seq=S),
        grid=(B, G, S // TQ),
        in_specs=[
            pl.BlockSpec((None, TQ, gw * LANES), lambda b, g, i: (b, i, g)),
            kvc, kvc, kvs(2), kvs(2), kvs(1), kvs(2),
            pl.BlockSpec((None, TQ, LANES), lambda b, g, i: (b, i, CB_NG)),
            pl.BlockSpec((None, TQ, gw * LANES), lambda b, g, i: (b, i, CB_NZ // gw + g)),
        ],
        out_specs=pl.BlockSpec((None, TQ, gw * LANES), lambda b, g, i: (b, i, g)),
        out_shape=jax.ShapeDtypeStruct((B, S, NSA_HEADS * HEAD_DIM), BF16),
        scratch_shapes=[pltpu.VMEM((S // SLC_TK, NSA_GROUP * TQ, SLC_TK), F32)],
        compiler_params=pltpu.CompilerParams(
            dimension_semantics=("parallel", "parallel", "arbitrary"), vmem_limit_bytes=VMEM_LIMIT),
        name="nsa_core",
    )(qn, kc, vc, ks, vs, kw, vw, proj_b, proj_b)


def _merge_kernel(a_ref, b_ref, ma_ref, mb_ref, x_ref, gate_ref, wa_ref, wb_ref, wo_ref, o_ref):
    ya = _dot(a_ref[...], wa_ref[...])
    yb = _dot(b_ref[...], wb_ref[...])
    y = _sigmoid(ma_ref[...]) * ya + _sigmoid(mb_ref[...]) * yb
    o_ref[...] = x_ref[...] + gate_ref[...] * _dot(y.astype(BF16), wo_ref[...])


def _merge(a2, b2, proj_b2, x2, mod_l, wa, wb, wo, seq):
    M, D = x2.shape
    tm = 256
    per_b = seq // tm
    B = mod_l.shape[1]
    mod4 = mod_l.reshape(3, B, 1, D)
    dcols = D // LANES
    full = lambda a: pl.BlockSpec(a.shape, lambda i: (0, 0))
    return pl.pallas_call(
        _merge_kernel,
        grid=(M // tm,),
        in_specs=[
            pl.BlockSpec((tm, a2.shape[1]), lambda i: (i, 0)),
            pl.BlockSpec((tm, b2.shape[1]), lambda i: (i, 0)),
            pl.BlockSpec((tm, D), lambda i: (i, CB_MA // dcols)),
            pl.BlockSpec((tm, D), lambda i: (i, CB_MB // dcols)),
            pl.BlockSpec((tm, D), lambda i: (i, 0)),
            pl.BlockSpec((None, None, 1, D), lambda i: (2, i // per_b, 0, 0)),
            full(wa), full(wb), full(wo),
        ],
        out_specs=pl.BlockSpec((tm, D), lambda i: (i, 0)),
        out_shape=jax.ShapeDtypeStruct((M, D), F32),
        compiler_params=pltpu.CompilerParams(
            dimension_semantics=("parallel",), vmem_limit_bytes=VMEM_LIMIT),
        name="merge_out",
    )(a2, b2, proj_b2, proj_b2, x2, mod4, wa, wb, wo)


def _permute_w_in(w_in):
    sb, nw, kv = SB_HEADS * HEAD_DIM, NSA_HEADS * HEAD_DIM, NSA_KV_HEADS * HEAD_DIM
    D = w_in.shape[1]
    sizes = (sb, sb, sb, sb, nw, kv, kv, kv, kv, kv, kv, nw, 3 * NSA_HEADS, D, D)
    offs = np.concatenate([[0], np.cumsum(sizes)])
    assert offs[-1] == w_in.shape[2]
    seg = lambda k: w_in[:, :, offs[k]:offs[k + 1]]
    pad = lambda n: jnp.zeros(w_in.shape[:2] + (n,), w_in.dtype)
    part_a = jnp.concatenate([seg(0) * Q_SCALE2, seg(1), seg(2), seg(3)], axis=-1)
    part_b = jnp.concatenate([seg(4), seg(11), seg(13), seg(14), seg(5), seg(6), seg(7), seg(8), seg(9),
                              seg(10), seg(12), pad(LANES - 3 * NSA_HEADS), pad(LANES)], axis=-1)
    assert part_a.shape[-1] == N_PROJ_A and part_b.shape[-1] == N_PROJ_B
    return part_a.astype(BF16), part_b.astype(BF16)


def _rope_tables(pos):
    half = HEAD_DIM // 2
    freq = ROPE_THETA ** (-jnp.arange(half, dtype=F32) / half)
    ang = pos.astype(F32)[:, None] * freq[None, :]
    cos, sin = jnp.cos(ang), jnp.sin(ang)
    reps = LANES // HEAD_DIM
    return jnp.tile(jnp.concatenate([cos, cos], -1), (1, reps)), jnp.tile(jnp.concatenate([-sin, sin], -1), (1, reps))


def kernel(x, c, ada_w, ada_b, norm_g, w_in, q_norm_g, kc_norm_g, ks_norm_g, kw_norm_g, cmp_pe_k, cmp_w1_k, cmp_w2_k, cmp_pe_v, cmp_w1_v, cmp_w2_v, w_proj_a, w_proj_b, w_out):
    B, S, D = x.shape
    L = ada_w.shape[0]
    M = B * S
    assert D == 8 * LANES and S % 512 == 0 and S // SLC_BLOCK >= SLC_TOPK and S >= WINDOW + TQ
    assert TQ == LANES and S // SLC_BLOCK <= LANES
    n_cmp_rows = S // CMP_STRIDE

    mod = _modulation(c, ada_w, ada_b)
    w_a, w_b = _permute_w_in(w_in)
    wa, wb, wo = w_proj_a.astype(BF16), w_proj_b.astype(BF16), w_out.astype(BF16)
    cos, sin = _rope_tables(jnp.arange(S))
    cos_c, sin_c = _rope_tables(CMP_STRIDE * jnp.arange(n_cmp_rows) + CMP_LEN - 1)
    tile2 = lambda g: jnp.tile(g, (1, LANES // HEAD_DIM)).reshape(L, 1, LANES)
    q_g, kc_g, ks_g, kw_g = tile2(q_norm_g), tile2(kc_norm_g), tile2(ks_norm_g), tile2(kw_norm_g)

    x2 = x.reshape(M, D)
    for l in range(L):
        proj_a = _in_projection(x2, mod[l], norm_g[l], w_a[l], S, BF16).reshape(B, S, N_PROJ_A)
        proj_b2 = _in_projection(x2, mod[l], norm_g[l], w_b[l], S, F32)
        proj_b = proj_b2.reshape(B, S, N_PROJ_B)
        a = _sb_attention(proj_a)
        qn, ks, vs, kw, vw = _nsa_prep(proj_b, cos, sin, q_g[l], ks_g[l], kw_g[l])
        xk = proj_b[:, :, CB_KC * LANES:(CB_KC + 1) * LANES].reshape(B, n_cmp_rows, CMP_STRIDE * LANES)
        xv = proj_b[:, :, CB_VC * LANES:(CB_VC + 1) * LANES].reshape(B, n_cmp_rows, CMP_STRIDE * LANES)
        kc, vc = _compress(xk, xv,
                           _expand_cmp_weights(cmp_pe_k[l], cmp_w1_k[l], cmp_w2_k[l]),
                           _expand_cmp_weights(cmp_pe_v[l], cmp_w1_v[l], cmp_w2_v[l]),
                           cos_c, sin_c, kc_g[l])
        b = _nsa_core(qn, kc, vc, ks, vs, kw, vw, proj_b)
        x2 = _merge(a.reshape(M, -1), b.reshape(M, -1), proj_b2, x2, mod[l], wa[l], wb[l], wo[l], S)
    return x2.reshape(B, S, D)
```

```python
import functools
import math

import jax
import jax.numpy as jnp
import numpy as np
from jax import lax
from jax.experimental import pallas as pl
from jax.experimental.pallas import tpu as pltpu

HEAD_DIM = 64
LANES = 128
SB_HEADS = 8
NSA_HEADS = 8
NSA_KV_HEADS = 2
NSA_GROUP = NSA_HEADS // NSA_KV_HEADS
CMP_LEN = 32
CMP_STRIDE = 16
CMP_HIDDEN = 128
SLC_BLOCK = 64
SLC_TOPK = 8
WINDOW = 256
ROPE_THETA = 10000.0
EPS = 1e-6
NEG = -1e30
FORCE = 1e4
Q_SCALE2 = HEAD_DIM ** -0.5 * math.log2(math.e)
BIG = 2.0 ** 100

F32 = jnp.float32
BF16 = jnp.bfloat16

CB_SBQ, CB_SBK, CB_SBV, CB_SBZ = 0, 4, 8, 12
N_PROJ_A = 16 * LANES
CB_NQ, CB_NZ, CB_MA, CB_MB = 0, 4, 8, 16
CB_KC, CB_VC, CB_KS, CB_VS, CB_KW, CB_VW, CB_NG = 24, 25, 26, 27, 28, 29, 30
N_PROJ_B = 32 * LANES

TQ = 128
SLC_TK = 256
SB_T = 256
SB_PAIRS = 4
VMEM_LIMIT = 56 * 1024 * 1024


def _dot(a, b):
    return jnp.dot(a, b, preferred_element_type=F32)


def _dot_nt(a, b):
    return lax.dot_general(a, b, (((1,), (1,)), ((), ())), preferred_element_type=F32)


def _split(x):
    hi = x.astype(BF16)
    lo = (x - hi.astype(F32)).astype(BF16)
    return hi, lo


def _dot_hl(x, m):
    hi, lo = _split(x)
    return _dot(hi, m) + _dot(lo, m)


def _dot3(a, b):
    ah, al = _split(a)
    bh, bl = _split(b)
    return _dot(ah, bh) + _dot(al, bh) + _dot(ah, bl)


def _sigmoid(x):
    return 1.0 / (1.0 + jnp.exp(-x))


def _silu(x):
    return x * _sigmoid(x)


def _iota(shape, dim):
    return lax.broadcasted_iota(jnp.int32, shape, dim)


def _head_block_ones():
    r = _iota((LANES, LANES), 0) // HEAD_DIM
    c = _iota((LANES, LANES), 1) // HEAD_DIM
    return (r == c).astype(BF16)


def _norm_rope(x, g, cos, sin_signed):
    ss = _dot_hl(x * x, _head_block_ones())
    y = x * lax.rsqrt(ss * (1.0 / HEAD_DIM) + EPS) * g
    lane = _iota(y.shape, 1)
    first_half = (lane % HEAD_DIM) < (HEAD_DIM // 2)
    swapped = jnp.where(first_half, pltpu.roll(y, LANES - HEAD_DIM // 2, 1),
                        pltpu.roll(y, HEAD_DIM // 2, 1))
    return y * cos + swapped * sin_signed


def _dup_halves(x):
    lane = _iota(x.shape, 1)
    r = pltpu.roll(x, HEAD_DIM, 1)
    low = lane < HEAD_DIM
    return jnp.where(low, x, r), jnp.where(low, r, x)


def _mod_kernel(c_ref, w_ref, b_ref, o_ref):
    o_ref[...] = _dot3(_silu(c_ref[...]), w_ref[...]) + b_ref[...]


def _modulation(c, ada_w, ada_b):
    L, D, _ = ada_w.shape
    B = c.shape[0]
    return pl.pallas_call(
        _mod_kernel,
        grid=(L, 3),
        in_specs=[
            pl.BlockSpec((B, D), lambda l, j: (0, 0)),
            pl.BlockSpec((None, D, D), lambda l, j: (l, 0, j)),
            pl.BlockSpec((None, 1, D), lambda l, j: (l, 0, j)),
        ],
        out_specs=pl.BlockSpec((None, None, B, D), lambda l, j: (l, j, 0, 0)),
        out_shape=jax.ShapeDtypeStruct((L, 3, B, D), F32),
        compiler_params=pltpu.CompilerParams(dimension_semantics=("arbitrary", "arbitrary")),
        name="adaln_mod",
    )(c, ada_w, ada_b.reshape(L, 1, 3 * D))


def _inproj_kernel(x_ref, shift_ref, scale_ref, g_ref, w_ref, o_ref, h_ref):
    @pl.when(pl.program_id(1) == 0)
    def _():
        x = x_ref[...]
        ms = jnp.mean(x * x, axis=-1, keepdims=True)
        y = x * lax.rsqrt(ms + EPS) * g_ref[...]
        h_ref[...] = (y * (1.0 + scale_ref[...]) + shift_ref[...]).astype(BF16)

    o_ref[...] = _dot(h_ref[...], w_ref[...]).astype(o_ref.dtype)


def _in_projection(x2, mod_l, norm_g, w_l, seq, out_dtype):
    M, D = x2.shape
    N = w_l.shape[1]
    tm, tn = 512, 1024
    per_b = seq // tm
    B = mod_l.shape[1]
    mod4 = mod_l.reshape(3, B, 1, D)
    return pl.pallas_call(
        _inproj_kernel,
        grid=(M // tm, N // tn),
        in_specs=[
            pl.BlockSpec((tm, D), lambda i, j: (i, 0)),
            pl.BlockSpec((None, None, 1, D), lambda i, j: (0, i // per_b, 0, 0)),
            pl.BlockSpec((None, None, 1, D), lambda i, j: (1, i // per_b, 0, 0)),
            pl.BlockSpec((1, D), lambda i, j: (0, 0)),
            pl.BlockSpec((D, tn), lambda i, j: (0, j)),
        ],
        out_specs=pl.BlockSpec((tm, tn), lambda i, j: (i, j)),
        out_shape=jax.ShapeDtypeStruct((M, N), out_dtype),
        scratch_shapes=[pltpu.VMEM((tm, D), BF16)],
        compiler_params=pltpu.CompilerParams(
            dimension_semantics=("parallel", "arbitrary"), vmem_limit_bytes=VMEM_LIMIT),
        name="in_projection",
    )(x2, mod4, mod4, norm_g.reshape(1, D), w_l)


def _sb_kernel(q_ref, k_ref, v_ref, z_ref, o_ref):
    T = SB_T
    i = pl.program_id(2)
    n_pair = q_ref.shape[1] // LANES
    lane = _iota((T, LANES), 1)
    low = lane < HEAD_DIM
    q_heads = []
    for p in range(n_pair):
        q2 = q_ref[:, p * LANES:(p + 1) * LANES]
        zero = jnp.zeros_like(q2)
        q_heads += [jnp.where(low, q2, zero), jnp.where(low, zero, q2)]

    suffix = (_iota((T, T), 0) > _iota((T, T), 1)).astype(BF16)
    strict = _iota((T, T), 1) < _iota((T, T), 0)

    def chain(qh, k, v, acc, carry, diagonal):
        z = _dot_nt(qh, k)
        log_keep = jnp.minimum(-z, 0.0) - jnp.log2(1.0 + jnp.exp2(-jnp.abs(z)))
        log_hit = z + log_keep
        if diagonal:
            log_keep = jnp.where(strict, log_keep, 0.0)
        keep_b = log_keep.astype(BF16)
        within = _dot(keep_b, suffix)
        w = jnp.exp2(log_hit + within + carry)
        if diagonal:
            w = jnp.where(strict, w, 0.0)
        row_total = within[:, 0:1] + keep_b[:, 0:1].astype(F32)
        return acc + _dot(w.astype(BF16), v), carry + row_total

    def block(kb, state, diagonal):
        rows = pl.ds(pl.multiple_of(kb * T, T), T)
        out = []
        for h in range(2 * n_pair):
            cols = slice((h // 2) * LANES, (h // 2 + 1) * LANES)
            out += chain(q_heads[h], k_ref[rows, cols], v_ref[rows, cols], state[2 * h], state[2 * h + 1], diagonal)
        return tuple(out)

    acc0 = jnp.zeros((T, LANES), F32)
    carry0 = jnp.zeros((T, 1), F32)
    state = block(i, (acc0, carry0) * (2 * n_pair), True)
    state = lax.fori_loop(0, i, lambda jj, st: block(i - 1 - jj, st, False), state)
    for p in range(n_pair):
        out = jnp.where(low, state[4 * p], state[4 * p + 2])
        cols = slice(p * LANES, (p + 1) * LANES)
        o_ref[:, cols] = (out * _silu(z_ref[:, cols].astype(F32))).astype(BF16)


def _sb_attention(proj_a):
    B, S, _ = proj_a.shape
    pairs = SB_HEADS // 2
    T = SB_T
    pp = SB_PAIRS
    w = pp * LANES
    return pl.pallas_call(
        _sb_kernel,
        grid=(B, pairs // pp, S // T),
        in_specs=[
            pl.BlockSpec((None, T, w), lambda b, p, i: (b, i, CB_SBQ // pp + p)),
            pl.BlockSpec((None, S, w), lambda b, p, i: (b, 0, CB_SBK // pp + p)),
            pl.BlockSpec((None, S, w), lambda b, p, i: (b, 0, CB_SBV // pp + p)),
            pl.BlockSpec((None, T, w), lambda b, p, i: (b, i, CB_SBZ // pp + p)),
        ],
        out_specs=pl.BlockSpec((None, T, w), lambda b, p, i: (b, i, p)),
        out_shape=jax.ShapeDtypeStruct((B, S, pairs * LANES), BF16),
        compiler_params=pltpu.CompilerParams(
            dimension_semantics=("parallel", "parallel", "arbitrary"), vmem_limit_bytes=VMEM_LIMIT),
        name="stick_breaking",
    )(proj_a, proj_a, proj_a, proj_a)


def _prep_kernel(q_ref, ks_ref, vs_ref, kw_ref, vw_ref, cos_ref, sin_ref, qg_ref, ksg_ref, kwg_ref,
                 qo_ref, kso_ref, vso_ref, kwo_ref, vwo_ref):
    tm = q_ref.shape[0]
    cos = cos_ref[...]
    sin = sin_ref[...]
    for p in range(NSA_HEADS // 2):
        x = q_ref[:, p * LANES:(p + 1) * LANES]
        qo_ref[:, p * LANES:(p + 1) * LANES] = (
            _norm_rope(x, qg_ref[...], cos, sin) * Q_SCALE2).astype(BF16)
    pos = pl.program_id(1) * tm + _iota((tm, LANES), 0)
    own_block = jnp.where(_iota((tm, LANES), 1) == pos // SLC_BLOCK, -BIG, 0.0).astype(BF16)
    ones = jnp.ones((tm, LANES), BF16)
    for g, half in enumerate(_dup_halves(_norm_rope(ks_ref[...], ksg_ref[...], cos, sin))):
        kso_ref[g] = jnp.concatenate([half.astype(BF16), own_block], axis=1)
    for g, half in enumerate(_dup_halves(_norm_rope(kw_ref[...], kwg_ref[...], cos, sin))):
        kwo_ref[g] = half.astype(BF16)
    for src, dst in ((vs_ref, vso_ref), (vw_ref, vwo_ref)):
        for g, half in enumerate(_dup_halves(src[...])):
            dst[g] = jnp.concatenate([half.astype(BF16), ones], axis=1)


def _nsa_prep(proj_b, cos, sin, q_g, ks_g, kw_g):
    B, S, _ = proj_b.shape
    tm = 256
    G = NSA_KV_HEADS
    col = lambda cb, w: pl.BlockSpec((None, tm, w * LANES), lambda b, i: (b, i, cb // w))
    tab = pl.BlockSpec((tm, LANES), lambda b, i: (i, 0))
    gain = pl.BlockSpec((1, LANES), lambda b, i: (0, 0))
    kv_out = lambda w: pl.BlockSpec((None, G, tm, w * LANES), lambda b, i: (b, 0, i, 0))
    kv_shape = lambda w: jax.ShapeDtypeStruct((B, G, S, w * LANES), BF16)
    return pl.pallas_call(
        _prep_kernel,
        grid=(B, S // tm),
        in_specs=[col(CB_NQ, 4), col(CB_KS, 1), col(CB_VS, 1), col(CB_KW, 1), col(CB_VW, 1),
                  tab, tab, gain, gain, gain],
        out_specs=[pl.BlockSpec((None, tm, 4 * LANES), lambda b, i: (b, i, 0)),
                   kv_out(2), kv_out(2), kv_out(1), kv_out(2)],
        out_shape=[jax.ShapeDtypeStruct((B, S, 4 * LANES), BF16),
                   kv_shape(2), kv_shape(2), kv_shape(1), kv_shape(2)],
        compiler_params=pltpu.CompilerParams(
            dimension_semantics=("parallel", "parallel"), vmem_limit_bytes=VMEM_LIMIT),
        name="nsa_prep",
    )(proj_b, proj_b, proj_b, proj_b, proj_b, cos, sin, q_g, ks_g, kw_g)


def _compress_kernel(xk_ref, xv_ref, pek_ref, pev_ref, w1k_ref, w1v_ref, w2k_ref, w2v_ref,
                     cos_ref, sin_ref, g_ref, ko_ref, vo_ref):
    nrow = xk_ref.shape[0]
    half = CMP_STRIDE * LANES

    def mlp(x_ref, pe_ref, w1_ref, w2_ref):
        x = x_ref[...]
        first = _dot((x + pe_ref[0:1, :]).astype(BF16), w1_ref[0:half, :])
        second = _dot((x + pe_ref[1:2, :]).astype(BF16), w1_ref[half:2 * half, :])
        pre = first + pltpu.roll(second, nrow - 1, 0)
        return _dot(_silu(pre).astype(BF16), w2_ref[...])

    kc = _norm_rope(mlp(xk_ref, pek_ref, w1k_ref, w2k_ref), g_ref[...], cos_ref[...], sin_ref[...])
    a, b = _dup_halves(kc)
    ko_ref[0] = a.astype(BF16)
    ko_ref[1] = b.astype(BF16)
    a, b = _dup_halves(mlp(xv_ref, pev_ref, w1v_ref, w2v_ref))
    vo_ref[0] = a.astype(BF16)
    vo_ref[1] = b.astype(BF16)


def _expand_cmp_weights(pe, w1, w2):
    G = NSA_KV_HEADS
    eye = jnp.eye(G, dtype=w1.dtype)
    w1h = w1.reshape(2, CMP_STRIDE, HEAD_DIM, CMP_HIDDEN)
    w1e = jnp.einsum('aldh,gk->algdkh', w1h, eye).reshape(2 * CMP_STRIDE * G * HEAD_DIM, G * CMP_HIDDEN)
    w2e = jnp.einsum('hd,gk->ghkd', w2, eye).reshape(G * CMP_HIDDEN, G * HEAD_DIM)
    pee = jnp.broadcast_to(pe.reshape(2, CMP_STRIDE, 1, HEAD_DIM), (2, CMP_STRIDE, G, HEAD_DIM))
    return pee.reshape(2, CMP_STRIDE * G * HEAD_DIM), w1e.astype(BF16), w2e.astype(BF16)


def _compress(xk, xv, wk, wv, cos_c, sin_c, kc_g):
    B, nrow, width = xk.shape
    G = NSA_KV_HEADS
    full = lambda a: pl.BlockSpec(a.shape, lambda b: (0,) * a.ndim)
    xspec = pl.BlockSpec((None, nrow, width), lambda b: (b, 0, 0))
    ospec = pl.BlockSpec((None, G, nrow, LANES), lambda b: (b, 0, 0, 0))
    oshape = jax.ShapeDtypeStruct((B, G, nrow, LANES), BF16)
    args = (xk, xv, wk[0], wv[0], wk[1], wv[1], wk[2], wv[2], cos_c, sin_c, kc_g)
    return pl.pallas_call(
        _compress_kernel,
        grid=(B,),
        in_specs=[xspec, xspec] + [full(a) for a in args[2:]],
        out_specs=[ospec, ospec],
        out_shape=[oshape, oshape],
        compiler_params=pltpu.CompilerParams(
            dimension_semantics=("parallel",), vmem_limit_bytes=VMEM_LIMIT),
        name="nsa_compress",
    )(*args)


def _nsa_kernel(q_ref, kc_ref, vc_ref, ks_ref, vs_ref, kw_ref, vw_ref, ng_ref, nz_ref, o_ref, s_ref, *, seq):
    i = pl.program_id(1)
    G = NSA_KV_HEADS
    R = NSA_GROUP * TQ
    n_pairs = NSA_GROUP // 2
    n_blocks = seq // SLC_BLOCK
    n_cmp = (seq - CMP_LEN) // CMP_STRIDE + 1
    nc = kc_ref.shape[1]
    t0 = i * TQ

    lane_q = _iota((TQ, LANES), 1)
    low = lane_q < HEAD_DIM
    zero = jnp.zeros((TQ, LANES), BF16)
    onehot_t = (_iota((R, LANES), 1) == _iota((R, LANES), 0) % TQ).astype(BF16)

    def masked_bias(allowed):
        return jnp.where(allowed, 0.0, -BIG).astype(BF16)

    n_i = _iota((nc, LANES), 0)
    t_i = t0 + _iota((nc, LANES), 1)
    bias_c = masked_bias((CMP_STRIDE * n_i + (CMP_LEN - 1) <= t_i) & (n_i < n_cmp))
    has_valid = t0 + _iota((R, nc), 0) % TQ >= CMP_LEN - 1

    span = WINDOW + TQ
    start = pl.multiple_of(jnp.clip(t0 - WINDOW, 0, seq - span), TQ)
    k_pos = start + _iota((span, LANES), 0)
    t_w = t0 + _iota((span, LANES), 1)
    bias_w = masked_bias((k_pos <= t_w) & (k_pos > t_w - WINDOW))

    jn = _iota((LANES, nc), 0)
    nn = _iota((LANES, nc), 1)
    overlap_t = ((CMP_STRIDE * nn < SLC_BLOCK * (jn + 1)) & (CMP_STRIDE * nn + CMP_LEN > SLC_BLOCK * jn)
                 & (nn < n_cmp) & (jn < n_blocks)).astype(BF16)
    nb8 = -(-n_blocks // 8) * 8
    jb = _iota((nb8, TQ), 0)
    cur = (t0 + _iota((nb8, TQ), 1)) // SLC_BLOCK
    eye = (_iota((TQ, TQ), 0) == _iota((TQ, TQ), 1)).astype(BF16)

    o_cmp_g, o_win_g, qs_sel_g = [], [], []
    for g in range(G):
        slabs = []
        for p in range(n_pairs):
            qp = q_ref[:, (g * n_pairs + p) * LANES:(g * n_pairs + p + 1) * LANES]
            slabs += [jnp.where(low, qp, zero), jnp.where(low, zero, qp)]
        qs = jnp.concatenate(slabs, axis=0)
        qs_pos = jnp.concatenate([qs, onehot_t], axis=1)

        s_c = _dot_nt(qs_pos, jnp.concatenate([kc_ref[g], bias_c], axis=1))
        e = jnp.exp2(s_c - jnp.max(s_c, axis=-1, keepdims=True))
        p_c = jnp.where(has_valid, e / jnp.sum(e, axis=-1, keepdims=True), 0.0)
        o_cmp_g.append(_dot(p_c.astype(BF16), vc_ref[g]))

        s_w = _dot_nt(qs_pos, jnp.concatenate([kw_ref[g, pl.ds(start, span), :], bias_w], axis=1))
        ew = jnp.exp2(s_w - jnp.max(s_w, axis=-1, keepdims=True)).astype(BF16)
        ow = _dot(ew, vw_ref[g, pl.ds(start, span), :])
        o_win_g.append(ow[:, :LANES] / ow[:, LANES:])

        p_sum = p_c[0:TQ]
        for h in range(1, NSA_GROUP):
            p_sum = p_sum + p_c[h * TQ:(h + 1) * TQ]
        p_hi, p_lo = _split(p_sum)
        imp = (_dot_nt(overlap_t, p_hi) + _dot_nt(overlap_t, p_lo))[:nb8]
        imp = jnp.where(jb > cur, NEG, imp)
        imp = jnp.where((jb == 0) | (jb == cur - 1), FORCE, imp)
        imp = jnp.where(jb == cur, 2.0 * FORCE, imp)
        rank = jnp.zeros((nb8, TQ), jnp.int32)
        for j in range(n_blocks):
            row = jnp.broadcast_to(imp[j:j + 1, :], (nb8, TQ))
            ahead = (row > imp) | ((row == imp) & (jb > j))
            rank = rank + ahead.astype(jnp.int32)
        chosen_t = (rank < min(SLC_TOPK, n_blocks)) & (jb <= cur) & (jb < n_blocks)
        not_t = jnp.where(chosen_t, 0.0, 1.0)
        if nb8 < LANES:
            not_t = jnp.concatenate([not_t, jnp.zeros((LANES - nb8, TQ), F32)], axis=0)
        not_chosen = _dot_nt(eye, not_t.astype(BF16)).astype(BF16)
        qs_sel_g.append(jnp.concatenate([qs, jnp.concatenate([not_chosen] * NSA_GROUP, axis=0)], axis=1))

    last = (t0 + TQ - 1) // SLC_TK

    def chunk(ref, g, c):
        return ref[g, pl.ds(pl.multiple_of(c * SLC_TK, SLC_TK), SLC_TK), :]

    def pass1(c, run):
        out = []
        for g in range(G):
            s = _dot_nt(qs_sel_g[g], chunk(ks_ref, g, c))
            s_ref[g, c] = s
            out.append(jnp.maximum(run[g], s))
        return tuple(out)

    run = lax.fori_loop(0, last, pass1, (jnp.full((R, SLC_TK), -3.0e38, F32),) * G)
    causal = last * SLC_TK + _iota((R, SLC_TK), 1) <= t0 + _iota((R, SLC_TK), 0) % TQ
    m_sel = []
    for g in range(G):
        s = jnp.where(causal, _dot_nt(qs_sel_g[g], chunk(ks_ref, g, last)), -BIG)
        s_ref[g, last] = s
        m_sel.append(jnp.broadcast_to(jnp.max(jnp.maximum(run[g], s), axis=-1, keepdims=True), (R, SLC_TK)))

    def pass2(c, acc):
        out = []
        for g in range(G):
            p = jnp.exp2(s_ref[g, c] - m_sel[g]).astype(BF16)
            out.append(acc[g] + _dot(p, chunk(vs_ref, g, c)))
        return tuple(out)

    acc = lax.fori_loop(0, last + 1, pass2, (jnp.zeros((R, 2 * LANES), F32),) * G)

    sig_hi, sig_lo = _split(_sigmoid(ng_ref[...]))
    width = 3 * n_pairs * LANES
    col = _iota((LANES, width), 1)
    for g in range(G):
        o_slc = acc[g][:, :LANES] / acc[g][:, LANES:]
        want = (col // (n_pairs * LANES)) * NSA_HEADS + g * NSA_GROUP + 2 * ((col // LANES) % n_pairs) \
            + (col % LANES) // HEAD_DIM
        pick = (_iota((LANES, width), 0) == want).astype(BF16)
        gates = _dot(sig_hi, pick) + _dot(sig_lo, pick)
        for p in range(n_pairs):
            total = None
            for br, o in enumerate((o_cmp_g[g], o_slc, o_win_g[g])):
                pair = jnp.where(low, o[2 * p * TQ:(2 * p + 1) * TQ], o[(2 * p + 1) * TQ:(2 * p + 2) * TQ])
                c0 = (br * n_pairs + p) * LANES
                term = gates[:, c0:c0 + LANES] * pair
                total = term if total is None else total + term
            cols = slice((g * n_pairs + p) * LANES, (g * n_pairs + p + 1) * LANES)
            o_ref[:, cols] = (total * _silu(nz_ref[:, cols])).astype(BF16)


def _nsa_core(qn, kc, vc, ks, vs, kw, vw, proj_b):
    B, S, _ = proj_b.shape
    G = NSA_KV_HEADS
    width = NSA_HEADS * HEAD_DIM
    ncmp = kc.shape[2]
    kvc = pl.BlockSpec((None, G, ncmp, LANES), lambda b, i: (b, 0, 0, 0))
    kvs = lambda w: pl.BlockSpec((None, G, S, w * LANES), lambda b, i: (b, 0, 0, 0))
    return pl.pallas_call(
        functools.partial(_nsa_kernel, seq=S),
        grid=(B, S // TQ),
        in_specs=[
            pl.BlockSpec((None, TQ, width), lambda b, i: (b, i, 0)),
            kvc, kvc, kvs(2), kvs(2), kvs(1), kvs(2),
            pl.BlockSpec((None, TQ, LANES), lambda b, i: (b, i, CB_NG)),
            pl.BlockSpec((None, TQ, width), lambda b, i: (b, i, CB_NZ * LANES // width)),
        ],
        out_specs=pl.BlockSpec((None, TQ, width), lambda b, i: (b, i, 0)),
        out_shape=jax.ShapeDtypeStruct((B, S, width), BF16),
        scratch_shapes=[pltpu.VMEM((G, S // SLC_TK, NSA_GROUP * TQ, SLC_TK), F32)],
        compiler_params=pltpu.CompilerParams(
            dimension_semantics=("parallel", "arbitrary"), vmem_limit_bytes=VMEM_LIMIT),
        name="nsa_core",
    )(qn, kc, vc, ks, vs, kw, vw, proj_b, proj_b)


def _merge_kernel(a_ref, b_ref, ma_ref, mb_ref, x_ref, gate_ref, wa_ref, wb_ref, wo_ref, o_ref):
    ya = _dot(a_ref[...], wa_ref[...])
    yb = _dot(b_ref[...], wb_ref[...])
    y = _sigmoid(ma_ref[...]) * ya + _sigmoid(mb_ref[...]) * yb
    o_ref[...] = x_ref[...] + gate_ref[...] * _dot(y.astype(BF16), wo_ref[...])


def _merge(a2, b2, proj_b2, x2, mod_l, wa, wb, wo, seq):
    M, D = x2.shape
    tm = 256
    per_b = seq // tm
    B = mod_l.shape[1]
    mod4 = mod_l.reshape(3, B, 1, D)
    dcols = D // LANES
    full = lambda a: pl.BlockSpec(a.shape, lambda i: (0, 0))
    return pl.pallas_call(
        _merge_kernel,
        grid=(M // tm,),
        in_specs=[
            pl.BlockSpec((tm, a2.shape[1]), lambda i: (i, 0)),
            pl.BlockSpec((tm, b2.shape[1]), lambda i: (i, 0)),
            pl.BlockSpec((tm, D), lambda i: (i, CB_MA // dcols)),
            pl.BlockSpec((tm, D), lambda i: (i, CB_MB // dcols)),
            pl.BlockSpec((tm, D), lambda i: (i, 0)),
            pl.BlockSpec((None, None, 1, D), lambda i: (2, i // per_b, 0, 0)),
            full(wa), full(wb), full(wo),
        ],
        out_specs=pl.BlockSpec((tm, D), lambda i: (i, 0)),
        out_shape=jax.ShapeDtypeStruct((M, D), F32),
        compiler_params=pltpu.CompilerParams(
            dimension_semantics=("parallel",), vmem_limit_bytes=VMEM_LIMIT),
        name="merge_out",
    )(a2, b2, proj_b2, proj_b2, x2, mod4, wa, wb, wo)


def _permute_w_in(w_in):
    sb, nw, kv = SB_HEADS * HEAD_DIM, NSA_HEADS * HEAD_DIM, NSA_KV_HEADS * HEAD_DIM
    D = w_in.shape[1]
    sizes = (sb, sb, sb, sb, nw, kv, kv, kv, kv, kv, kv, nw, 3 * NSA_HEADS, D, D)
    offs = np.concatenate([[0], np.cumsum(sizes)])
    assert offs[-1] == w_in.shape[2]
    seg = lambda k: w_in[:, :, offs[k]:offs[k + 1]]
    pad = lambda n: jnp.zeros(w_in.shape[:2] + (n,), w_in.dtype)
    part_a = jnp.concatenate([seg(0) * Q_SCALE2, seg(1), seg(2), seg(3)], axis=-1)
    part_b = jnp.concatenate([seg(4), seg(11), seg(13), seg(14), seg(5), seg(6), seg(7), seg(8), seg(9),
                              seg(10), seg(12), pad(LANES - 3 * NSA_HEADS), pad(LANES)], axis=-1)
    assert part_a.shape[-1] == N_PROJ_A and part_b.shape[-1] == N_PROJ_B
    return part_a.astype(BF16), part_b.astype(BF16)


def _rope_tables(pos):
    half = HEAD_DIM // 2
    freq = ROPE_THETA ** (-jnp.arange(half, dtype=F32) / half)
    ang = pos.astype(F32)[:, None] * freq[None, :]
    cos, sin = jnp.cos(ang), jnp.sin(ang)
    reps = LANES // HEAD_DIM
    return jnp.tile(jnp.concatenate([cos, cos], -1), (1, reps)), jnp.tile(jnp.concatenate([-sin, sin], -1), (1, reps))


def kernel(x, c, ada_w, ada_b, norm_g, w_in, q_norm_g, kc_norm_g, ks_norm_g, kw_norm_g, cmp_pe_k, cmp_w1_k, cmp_w2_k, cmp_pe_v, cmp_w1_v, cmp_w2_v, w_proj_a, w_proj_b, w_out):
    B, S, D = x.shape
    L = ada_w.shape[0]
    M = B * S
    assert D == 8 * LANES and S % 512 == 0 and S // SLC_BLOCK >= SLC_TOPK and S >= WINDOW + TQ
    assert TQ == LANES and S // SLC_BLOCK <= LANES
    n_cmp_rows = S // CMP_STRIDE

    mod = _modulation(c, ada_w, ada_b)
    w_a, w_b = _permute_w_in(w_in)
    wa, wb, wo = w_proj_a.astype(BF16), w_proj_b.astype(BF16), w_out.astype(BF16)
    cos, sin = _rope_tables(jnp.arange(S))
    cos_c, sin_c = _rope_tables(CMP_STRIDE * jnp.arange(n_cmp_rows) + CMP_LEN - 1)
    tile2 = lambda g: jnp.tile(g, (1, LANES // HEAD_DIM)).reshape(L, 1, LANES)
    q_g, kc_g, ks_g, kw_g = tile2(q_norm_g), tile2(kc_norm_g), tile2(ks_norm_g), tile2(kw_norm_g)

    x2 = x.reshape(M, D)
    for l in range(L):
        proj_a = _in_projection(x2, mod[l], norm_g[l], w_a[l], S, BF16).reshape(B, S, N_PROJ_A)
        proj_b2 = _in_projection(x2, mod[l], norm_g[l], w_b[l], S, F32)
        proj_b = proj_b2.reshape(B, S, N_PROJ_B)
        a = _sb_attention(proj_a)
        qn, ks, vs, kw, vw = _nsa_prep(proj_b, cos, sin, q_g[l], ks_g[l], kw_g[l])
        xk = proj_b[:, :, CB_KC * LANES:(CB_KC + 1) * LANES].reshape(B, n_cmp_rows, CMP_STRIDE * LANES)
        xv = proj_b[:, :, CB_VC * LANES:(CB_VC + 1) * LANES].reshape(B, n_cmp_rows, CMP_STRIDE * LANES)
        kc, vc = _compress(xk, xv,
                           _expand_cmp_weights(cmp_pe_k[l], cmp_w1_k[l], cmp_w2_k[l]),
                           _expand_cmp_weights(cmp_pe_v[l], cmp_w1_v[l], cmp_w2_v[l]),
                           cos_c, sin_c, kc_g[l])
        b = _nsa_core(qn, kc, vc, ks, vs, kw, vw, proj_b)
        x2 = _merge(a.reshape(M, -1), b.reshape(M, -1), proj_b2, x2, mod[l], wa[l], wb[l], wo[l], S)
    return x2.reshape(B, S, D)
```

```python
import functools
import math

import jax
import jax.numpy as jnp
import numpy as np
from jax import lax
from jax.experimental import pallas as pl
from jax.experimental.pallas import tpu as pltpu

HEAD_DIM = 64
LANES = 128
SB_HEADS = 8
NSA_HEADS = 8
NSA_KV_HEADS = 2
NSA_GROUP = NSA_HEADS // NSA_KV_HEADS
CMP_LEN = 32
CMP_STRIDE = 16
CMP_HIDDEN = 128
SLC_BLOCK = 64
SLC_TOPK = 8
WINDOW = 256
ROPE_THETA = 10000.0
EPS = 1e-6
NEG = -1e30
FORCE = 1e4
Q_SCALE2 = HEAD_DIM ** -0.5 * math.log2(math.e)
BIG = 2.0 ** 100

F32 = jnp.float32
BF16 = jnp.bfloat16

CB_SBQ, CB_SBK, CB_SBV, CB_SBZ = 0, 4, 8, 12
CB_MA, CB_MB, CB_NZ, CB_VC, CB_VS, CB_VW = 16, 24, 32, 36, 37, 38
N_PROJ_A = 40 * LANES
CB_NQ, CB_KC, CB_KS, CB_KW, CB_NG = 0, 4, 5, 6, 7
N_PROJ_B = 8 * LANES

TQ = 128
SLC_TK = 256
SB_T = 256
SB_PAIRS = 4
VMEM_LIMIT = 56 * 1024 * 1024


def _dot(a, b):
    return jnp.dot(a, b, preferred_element_type=F32)


def _dot_nt(a, b):
    return lax.dot_general(a, b, (((1,), (1,)), ((), ())), preferred_element_type=F32)


def _split(x):
    hi = x.astype(BF16)
    lo = (x - hi.astype(F32)).astype(BF16)
    return hi, lo


def _dot_hl(x, m):
    hi, lo = _split(x)
    return _dot(hi, m) + _dot(lo, m)


def _dot3(a, b):
    ah, al = _split(a)
    bh, bl = _split(b)
    return _dot(ah, bh) + _dot(al, bh) + _dot(ah, bl)


def _sigmoid(x):
    return 1.0 / (1.0 + jnp.exp(-x))


def _silu(x):
    return x * _sigmoid(x)


def _iota(shape, dim):
    return lax.broadcasted_iota(jnp.int32, shape, dim)


def _head_block_ones():
    r = _iota((LANES, LANES), 0) // HEAD_DIM
    c = _iota((LANES, LANES), 1) // HEAD_DIM
    return (r == c).astype(BF16)


def _norm_rope(x, g, cos, sin_signed):
    ss = _dot_hl(x * x, _head_block_ones())
    y = x * lax.rsqrt(ss * (1.0 / HEAD_DIM) + EPS) * g
    lane = _iota(y.shape, 1)
    first_half = (lane % HEAD_DIM) < (HEAD_DIM // 2)
    swapped = jnp.where(first_half, pltpu.roll(y, LANES - HEAD_DIM // 2, 1),
                        pltpu.roll(y, HEAD_DIM // 2, 1))
    return y * cos + swapped * sin_signed


def _dup_halves(x):
    lane = _iota(x.shape, 1)
    r = pltpu.roll(x, HEAD_DIM, 1)
    low = lane < HEAD_DIM
    return jnp.where(low, x, r), jnp.where(low, r, x)


def _mod_kernel(c_ref, w_ref, b_ref, o_ref):
    o_ref[...] = _dot3(_silu(c_ref[...]), w_ref[...]) + b_ref[...]


def _modulation(c, ada_w, ada_b):
    L, D, _ = ada_w.shape
    B = c.shape[0]
    return pl.pallas_call(
        _mod_kernel,
        grid=(L, 3),
        in_specs=[
            pl.BlockSpec((B, D), lambda l, j: (0, 0)),
            pl.BlockSpec((None, D, D), lambda l, j: (l, 0, j)),
            pl.BlockSpec((None, 1, D), lambda l, j: (l, 0, j)),
        ],
        out_specs=pl.BlockSpec((None, None, B, D), lambda l, j: (l, j, 0, 0)),
        out_shape=jax.ShapeDtypeStruct((L, 3, B, D), F32),
        compiler_params=pltpu.CompilerParams(dimension_semantics=("arbitrary", "arbitrary")),
        name="adaln_mod",
    )(c, ada_w, ada_b.reshape(L, 1, 3 * D))


def _inproj_kernel(x_ref, shift_ref, scale_ref, g_ref, w_ref, oa_ref, ob_ref, h_ref):
    j = pl.program_id(1)
    n_a = pl.num_programs(1) - 1

    @pl.when(j == 0)
    def _():
        x = x_ref[...]
        ms = jnp.mean(x * x, axis=-1, keepdims=True)
        y = x * lax.rsqrt(ms + EPS) * g_ref[...]
        h_ref[...] = (y * (1.0 + scale_ref[...]) + shift_ref[...]).astype(BF16)

    @pl.when(j < n_a)
    def _():
        oa_ref[...] = _dot(h_ref[...], w_ref[...]).astype(BF16)

    @pl.when(j == n_a)
    def _():
        ob_ref[...] = _dot(h_ref[...], w_ref[...])


def _in_projection(x2, mod_l, norm_g, w_l, seq):
    M, D = x2.shape
    tm, tn = 1024, N_PROJ_B
    n_a = N_PROJ_A // tn
    per_b = seq // tm
    B = mod_l.shape[1]
    mod4 = mod_l.reshape(3, B, 1, D)
    return pl.pallas_call(
        _inproj_kernel,
        grid=(M // tm, n_a + 1),
        in_specs=[
            pl.BlockSpec((tm, D), lambda i, j: (i, 0)),
            pl.BlockSpec((None, None, 1, D), lambda i, j: (0, i // per_b, 0, 0)),
            pl.BlockSpec((None, None, 1, D), lambda i, j: (1, i // per_b, 0, 0)),
            pl.BlockSpec((1, D), lambda i, j: (0, 0)),
            pl.BlockSpec((D, tn), lambda i, j: (0, j)),
        ],
        out_specs=[pl.BlockSpec((tm, tn), lambda i, j: (i, jnp.minimum(j, n_a - 1))),
                   pl.BlockSpec((tm, tn), lambda i, j: (i, 0))],
        out_shape=[jax.ShapeDtypeStruct((M, N_PROJ_A), BF16), jax.ShapeDtypeStruct((M, N_PROJ_B), F32)],
        scratch_shapes=[pltpu.VMEM((tm, D), BF16)],
        compiler_params=pltpu.CompilerParams(
            dimension_semantics=("parallel", "arbitrary"), vmem_limit_bytes=VMEM_LIMIT),
        name="in_projection",
    )(x2, mod4, mod4, norm_g.reshape(1, D), w_l)


def _sb_kernel(q_ref, k_ref, v_ref, z_ref, o_ref):
    T = SB_T
    i = pl.program_id(2)
    n_pair = q_ref.shape[1] // LANES
    lane = _iota((T, LANES), 1)
    low = lane < HEAD_DIM
    q_heads = []
    for p in range(n_pair):
        q2 = q_ref[:, p * LANES:(p + 1) * LANES]
        zero = jnp.zeros_like(q2)
        q_heads += [jnp.where(low, q2, zero), jnp.where(low, zero, q2)]

    suffix = (_iota((T, T), 0) > _iota((T, T), 1)).astype(BF16)
    strict = _iota((T, T), 1) < _iota((T, T), 0)

    def chain(qh, k, v, acc, carry, diagonal):
        z = _dot_nt(qh, k)
        nz = -z
        log_keep = jnp.minimum(nz, 0.0) - jnp.log2(1.0 + jnp.exp2(jnp.minimum(z, nz)))
        log_hit = z + log_keep
        if diagonal:
            log_keep = jnp.where(strict, log_keep, 0.0)
        keep_b = log_keep.astype(BF16)
        within = _dot(keep_b, suffix)
        w = jnp.exp2(log_hit + within + carry)
        if diagonal:
            w = jnp.where(strict, w, 0.0)
        row_total = within[:, 0:1] + keep_b[:, 0:1].astype(F32)
        return acc + _dot(w.astype(BF16), v), carry + row_total

    def block(kb, state, diagonal):
        rows = pl.ds(pl.multiple_of(kb * T, T), T)
        out = []
        for h in range(2 * n_pair):
            cols = slice((h // 2) * LANES, (h // 2 + 1) * LANES)
            out += chain(q_heads[h], k_ref[rows, cols], v_ref[rows, cols], state[2 * h], state[2 * h + 1], diagonal)
        return tuple(out)

    acc0 = jnp.zeros((T, LANES), F32)
    carry0 = jnp.zeros((T, 1), F32)
    state = block(i, (acc0, carry0) * (2 * n_pair), True)
    state = lax.fori_loop(0, i, lambda jj, st: block(i - 1 - jj, st, False), state)
    for p in range(n_pair):
        out = jnp.where(low, state[4 * p], state[4 * p + 2])
        cols = slice(p * LANES, (p + 1) * LANES)
        o_ref[:, cols] = (out * _silu(z_ref[:, cols].astype(F32))).astype(BF16)


def _sb_attention(proj_a):
    B, S, _ = proj_a.shape
    pairs = SB_HEADS // 2
    T = SB_T
    pp = SB_PAIRS
    w = pp * LANES
    return pl.pallas_call(
        _sb_kernel,
        grid=(B, pairs // pp, S // T),
        in_specs=[
            pl.BlockSpec((None, T, w), lambda b, p, i: (b, i, CB_SBQ // pp + p)),
            pl.BlockSpec((None, S, w), lambda b, p, i: (b, 0, CB_SBK // pp + p)),
            pl.BlockSpec((None, S, w), lambda b, p, i: (b, 0, CB_SBV // pp + p)),
            pl.BlockSpec((None, T, w), lambda b, p, i: (b, i, CB_SBZ // pp + p)),
        ],
        out_specs=pl.BlockSpec((None, T, w), lambda b, p, i: (b, i, p)),
        out_shape=jax.ShapeDtypeStruct((B, S, pairs * LANES), BF16),
        compiler_params=pltpu.CompilerParams(
            dimension_semantics=("parallel", "parallel", "arbitrary"), vmem_limit_bytes=VMEM_LIMIT),
        name="stick_breaking",
    )(proj_a, proj_a, proj_a, proj_a)


def _prep_kernel(q_ref, ks_ref, vs_ref, kw_ref, vw_ref, cos_ref, sin_ref, qg_ref, ksg_ref, kwg_ref,
                 qo_ref, kso_ref, vso_ref, kwo_ref, vwo_ref):
    tm = q_ref.shape[0]
    cos = cos_ref[...]
    sin = sin_ref[...]
    for p in range(NSA_HEADS // 2):
        x = q_ref[:, p * LANES:(p + 1) * LANES]
        qo_ref[:, p * LANES:(p + 1) * LANES] = (
            _norm_rope(x, qg_ref[...], cos, sin) * Q_SCALE2).astype(BF16)
    pos = pl.program_id(1) * tm + _iota((tm, LANES), 0)
    own_block = jnp.where(_iota((tm, LANES), 1) == pos // SLC_BLOCK, -BIG, 0.0).astype(BF16)
    ones = jnp.ones((tm, LANES), BF16)
    for g, half in enumerate(_dup_halves(_norm_rope(ks_ref[...], ksg_ref[...], cos, sin))):
        kso_ref[g] = jnp.concatenate([half.astype(BF16), own_block], axis=1)
    for g, half in enumerate(_dup_halves(_norm_rope(kw_ref[...], kwg_ref[...], cos, sin))):
        kwo_ref[g] = half.astype(BF16)
    for src, dst in ((vs_ref, vso_ref), (vw_ref, vwo_ref)):
        for g, half in enumerate(_dup_halves(src[...].astype(F32))):
            dst[g] = jnp.concatenate([half.astype(BF16), ones], axis=1)


def _nsa_prep(proj_a, proj_b, cos, sin, q_g, ks_g, kw_g):
    B, S, _ = proj_b.shape
    tm = 256
    G = NSA_KV_HEADS
    col = lambda cb, w: pl.BlockSpec((None, tm, w * LANES), lambda b, i: (b, i, cb // w))
    tab = pl.BlockSpec((tm, LANES), lambda b, i: (i, 0))
    gain = pl.BlockSpec((1, LANES), lambda b, i: (0, 0))
    kv_out = lambda w: pl.BlockSpec((None, G, tm, w * LANES), lambda b, i: (b, 0, i, 0))
    kv_shape = lambda w: jax.ShapeDtypeStruct((B, G, S, w * LANES), BF16)
    return pl.pallas_call(
        _prep_kernel,
        grid=(B, S // tm),
        in_specs=[col(CB_NQ, 4), col(CB_KS, 1), col(CB_VS, 1), col(CB_KW, 1), col(CB_VW, 1),
                  tab, tab, gain, gain, gain],
        out_specs=[pl.BlockSpec((None, tm, 4 * LANES), lambda b, i: (b, i, 0)),
                   kv_out(2), kv_out(2), kv_out(1), kv_out(2)],
        out_shape=[jax.ShapeDtypeStruct((B, S, 4 * LANES), BF16),
                   kv_shape(2), kv_shape(2), kv_shape(1), kv_shape(2)],
        compiler_params=pltpu.CompilerParams(
            dimension_semantics=("parallel", "parallel"), vmem_limit_bytes=VMEM_LIMIT),
        name="nsa_prep",
    )(proj_b, proj_b, proj_a, proj_b, proj_a, cos, sin, q_g, ks_g, kw_g)


def _compress_kernel(xk_ref, xv_ref, pek_ref, pev_ref, w1k_ref, w1v_ref, w2k_ref, w2v_ref,
                     cos_ref, sin_ref, g_ref, ko_ref, vo_ref):
    nrow = xk_ref.shape[0]
    half = CMP_STRIDE * LANES

    def mlp(x_ref, pe_ref, w1_ref, w2_ref):
        x = x_ref[...]
        first = _dot((x + pe_ref[0:1, :]).astype(BF16), w1_ref[0:half, :])
        second = _dot((x + pe_ref[1:2, :]).astype(BF16), w1_ref[half:2 * half, :])
        pre = first + pltpu.roll(second, nrow - 1, 0)
        return _dot(_silu(pre).astype(BF16), w2_ref[...])

    kc = _norm_rope(mlp(xk_ref, pek_ref, w1k_ref, w2k_ref), g_ref[...], cos_ref[...], sin_ref[...])
    a, b = _dup_halves(kc)
    ko_ref[0] = a.astype(BF16)
    ko_ref[1] = b.astype(BF16)
    a, b = _dup_halves(mlp(xv_ref, pev_ref, w1v_ref, w2v_ref))
    vo_ref[0] = a.astype(BF16)
    vo_ref[1] = b.astype(BF16)


def _expand_cmp_weights(pe, w1, w2):
    G = NSA_KV_HEADS
    eye = jnp.eye(G, dtype=w1.dtype)
    w1h = w1.reshape(2, CMP_STRIDE, HEAD_DIM, CMP_HIDDEN)
    w1e = jnp.einsum('aldh,gk->algdkh', w1h, eye).reshape(2 * CMP_STRIDE * G * HEAD_DIM, G * CMP_HIDDEN)
    w2e = jnp.einsum('hd,gk->ghkd', w2, eye).reshape(G * CMP_HIDDEN, G * HEAD_DIM)
    pee = jnp.broadcast_to(pe.reshape(2, CMP_STRIDE, 1, HEAD_DIM), (2, CMP_STRIDE, G, HEAD_DIM))
    return pee.reshape(2, CMP_STRIDE * G * HEAD_DIM), w1e.astype(BF16), w2e.astype(BF16)


def _compress(xk, xv, wk, wv, cos_c, sin_c, kc_g):
    B, nrow, width = xk.shape
    G = NSA_KV_HEADS
    full = lambda a: pl.BlockSpec(a.shape, lambda b: (0,) * a.ndim)
    xspec = pl.BlockSpec((None, nrow, width), lambda b: (b, 0, 0))
    ospec = pl.BlockSpec((None, G, nrow, LANES), lambda b: (b, 0, 0, 0))
    oshape = jax.ShapeDtypeStruct((B, G, nrow, LANES), BF16)
    args = (xk, xv, wk[0], wv[0], wk[1], wv[1], wk[2], wv[2], cos_c, sin_c, kc_g)
    return pl.pallas_call(
        _compress_kernel,
        grid=(B,),
        in_specs=[xspec, xspec] + [full(a) for a in args[2:]],
        out_specs=[ospec, ospec],
        out_shape=[oshape, oshape],
        compiler_params=pltpu.CompilerParams(
            dimension_semantics=("parallel",), vmem_limit_bytes=VMEM_LIMIT),
        name="nsa_compress",
    )(*args)


def _nsa_kernel(q_ref, kc_ref, vc_ref, ks_ref, vs_ref, kw_ref, vw_ref, ng_ref, nz_ref, o_ref, s_ref, *, seq):
    i = pl.program_id(1)
    G = NSA_KV_HEADS
    R = NSA_GROUP * TQ
    n_pairs = NSA_GROUP // 2
    n_blocks = seq // SLC_BLOCK
    n_cmp = (seq - CMP_LEN) // CMP_STRIDE + 1
    nc = kc_ref.shape[1]
    t0 = i * TQ

    lane_q = _iota((TQ, LANES), 1)
    low = lane_q < HEAD_DIM
    zero = jnp.zeros((TQ, LANES), BF16)
    onehot_t = (_iota((R, LANES), 1) == _iota((R, LANES), 0) % TQ).astype(BF16)

    def masked_bias(allowed):
        return jnp.where(allowed, 0.0, -BIG).astype(BF16)

    n_i = _iota((nc, LANES), 0)
    t_i = t0 + _iota((nc, LANES), 1)
    bias_c = masked_bias((CMP_STRIDE * n_i + (CMP_LEN - 1) <= t_i) & (n_i < n_cmp))
    has_valid = t0 + _iota((R, nc), 0) % TQ >= CMP_LEN - 1

    span = WINDOW + TQ
    start = pl.multiple_of(jnp.clip(t0 - WINDOW, 0, seq - span), TQ)
    k_pos = start + _iota((span, LANES), 0)
    t_w = t0 + _iota((span, LANES), 1)
    bias_w = masked_bias((k_pos <= t_w) & (k_pos > t_w - WINDOW))

    jn = _iota((LANES, nc), 0)
    nn = _iota((LANES, nc), 1)
    overlap_t = ((CMP_STRIDE * nn < SLC_BLOCK * (jn + 1)) & (CMP_STRIDE * nn + CMP_LEN > SLC_BLOCK * jn)
                 & (nn < n_cmp) & (jn < n_blocks)).astype(BF16)
    nb8 = -(-n_blocks // 8) * 8
    jb = _iota((nb8, TQ), 0)
    cur = (t0 + _iota((nb8, TQ), 1)) // SLC_BLOCK
    eye = (_iota((TQ, TQ), 0) == _iota((TQ, TQ), 1)).astype(BF16)

    o_cmp_g, o_win_g, qs_sel_g = [], [], []
    for g in range(G):
        slabs = []
        for p in range(n_pairs):
            qp = q_ref[:, (g * n_pairs + p) * LANES:(g * n_pairs + p + 1) * LANES]
            slabs += [jnp.where(low, qp, zero), jnp.where(low, zero, qp)]
        qs = jnp.concatenate(slabs, axis=0)
        qs_pos = jnp.concatenate([qs, onehot_t], axis=1)

        s_c = _dot_nt(qs_pos, jnp.concatenate([kc_ref[g], bias_c], axis=1))
        e = jnp.exp2(s_c - jnp.max(s_c, axis=-1, keepdims=True))
        p_c = jnp.where(has_valid, e / jnp.sum(e, axis=-1, keepdims=True), 0.0)
        o_cmp_g.append(_dot(p_c.astype(BF16), vc_ref[g]))

        s_w = _dot_nt(qs_pos, jnp.concatenate([kw_ref[g, pl.ds(start, span), :], bias_w], axis=1))
        ew = jnp.exp2(s_w - jnp.max(s_w, axis=-1, keepdims=True)).astype(BF16)
        ow = _dot(ew, vw_ref[g, pl.ds(start, span), :])
        o_win_g.append(ow[:, :LANES] / ow[:, LANES:])

        p_sum = p_c[0:TQ]
        for h in range(1, NSA_GROUP):
            p_sum = p_sum + p_c[h * TQ:(h + 1) * TQ]
        p_hi, p_lo = _split(p_sum)
        imp = (_dot_nt(overlap_t, p_hi) + _dot_nt(overlap_t, p_lo))[:nb8]
        imp = jnp.where(jb > cur, NEG, imp)
        imp = jnp.where((jb == 0) | (jb == cur - 1), FORCE, imp)
        imp = jnp.where(jb == cur, 2.0 * FORCE, imp)
        rank = jnp.zeros((nb8, TQ), jnp.int32)
        for j in range(n_blocks):
            row = jnp.broadcast_to(imp[j:j + 1, :], (nb8, TQ))
            ahead = (row > imp) | ((row == imp) & (jb > j))
            rank = rank + ahead.astype(jnp.int32)
        chosen_t = (rank < min(SLC_TOPK, n_blocks)) & (jb <= cur) & (jb < n_blocks)
        not_t = jnp.where(chosen_t, 0.0, 1.0)
        if nb8 < LANES:
            not_t = jnp.concatenate([not_t, jnp.zeros((LANES - nb8, TQ), F32)], axis=0)
        not_chosen = _dot_nt(eye, not_t.astype(BF16)).astype(BF16)
        qs_sel_g.append(jnp.concatenate([qs, jnp.concatenate([not_chosen] * NSA_GROUP, axis=0)], axis=1))

    last = (t0 + TQ - 1) // SLC_TK

    def chunk(ref, g, c):
        return ref[g, pl.ds(pl.multiple_of(c * SLC_TK, SLC_TK), SLC_TK), :]

    def fold(s):
        out = s[:, :LANES]
        for k in range(1, SLC_TK // LANES):
            out = jnp.maximum(out, s[:, k * LANES:(k + 1) * LANES])
        return out

    def pass1(c, run):
        out = []
        for g in range(G):
            s = _dot_nt(qs_sel_g[g], chunk(ks_ref, g, c))
            s_ref[g, c] = s
            out.append(jnp.maximum(run[g], fold(s)))
        return tuple(out)

    run = lax.fori_loop(0, last, pass1, (jnp.full((R, LANES), -3.0e38, F32),) * G)
    causal = last * SLC_TK + _iota((R, SLC_TK), 1) <= t0 + _iota((R, SLC_TK), 0) % TQ
    m_sel = []
    for g in range(G):
        s = jnp.where(causal, _dot_nt(qs_sel_g[g], chunk(ks_ref, g, last)), -BIG)
        s_ref[g, last] = s
        m_sel.append(jnp.broadcast_to(jnp.max(jnp.maximum(run[g], fold(s)), axis=-1, keepdims=True), (R, SLC_TK)))

    def pass2(c, acc):
        out = []
        for g in range(G):
            p = jnp.exp2(s_ref[g, c] - m_sel[g]).astype(BF16)
            out.append(acc[g] + _dot(p, chunk(vs_ref, g, c)))
        return tuple(out)

    acc = lax.fori_loop(0, last + 1, pass2, (jnp.zeros((R, 2 * LANES), F32),) * G)

    sig_hi, sig_lo = _split(_sigmoid(ng_ref[...]))
    width = 3 * n_pairs * LANES
    col = _iota((LANES, width), 1)
    for g in range(G):
        o_slc = acc[g][:, :LANES] / acc[g][:, LANES:]
        want = (col // (n_pairs * LANES)) * NSA_HEADS + g * NSA_GROUP + 2 * ((col // LANES) % n_pairs) \
            + (col % LANES) // HEAD_DIM
        pick = (_iota((LANES, width), 0) == want).astype(BF16)
        gates = _dot(sig_hi, pick) + _dot(sig_lo, pick)
        for p in range(n_pairs):
            total = None
            for br, o in enumerate((o_cmp_g[g], o_slc, o_win_g[g])):
                pair = jnp.where(low, o[2 * p * TQ:(2 * p + 1) * TQ], o[(2 * p + 1) * TQ:(2 * p + 2) * TQ])
                c0 = (br * n_pairs + p) * LANES
                term = gates[:, c0:c0 + LANES] * pair
                total = term if total is None else total + term
            cols = slice((g * n_pairs + p) * LANES, (g * n_pairs + p + 1) * LANES)
            o_ref[:, cols] = (total * _silu(nz_ref[:, cols].astype(F32))).astype(BF16)


def _nsa_core(qn, kc, vc, ks, vs, kw, vw, proj_a, proj_b):
    B, S, _ = proj_b.shape
    G = NSA_KV_HEADS
    width = NSA_HEADS * HEAD_DIM
    ncmp = kc.shape[2]
    kvc = pl.BlockSpec((None, G, ncmp, LANES), lambda b, i: (b, 0, 0, 0))
    kvs = lambda w: pl.BlockSpec((None, G, S, w * LANES), lambda b, i: (b, 0, 0, 0))
    return pl.pallas_call(
        functools.partial(_nsa_kernel, seq=S),
        grid=(B, S // TQ),
        in_specs=[
            pl.BlockSpec((None, TQ, width), lambda b, i: (b, i, 0)),
            kvc, kvc, kvs(2), kvs(2), kvs(1), kvs(2),
            pl.BlockSpec((None, TQ, LANES), lambda b, i: (b, i, CB_NG)),
            pl.BlockSpec((None, TQ, width), lambda b, i: (b, i, CB_NZ * LANES // width)),
        ],
        out_specs=pl.BlockSpec((None, TQ, width), lambda b, i: (b, i, 0)),
        out_shape=jax.ShapeDtypeStruct((B, S, width), BF16),
        scratch_shapes=[pltpu.VMEM((G, S // SLC_TK, NSA_GROUP * TQ, SLC_TK), F32)],
        compiler_params=pltpu.CompilerParams(
            dimension_semantics=("parallel", "arbitrary"), vmem_limit_bytes=VMEM_LIMIT),
        name="nsa_core",
    )(qn, kc, vc, ks, vs, kw, vw, proj_b, proj_a)


def _merge_kernel(a_ref, b_ref, ma_ref, mb_ref, x_ref, gate_ref, wa_ref, wb_ref, wo_ref, o_ref):
    ya = _dot(a_ref[...], wa_ref[...])
    yb = _dot(b_ref[...], wb_ref[...])
    y = _sigmoid(ma_ref[...].astype(F32)) * ya + _sigmoid(mb_ref[...].astype(F32)) * yb
    o_ref[...] = x_ref[...] + gate_ref[...] * _dot(y.astype(BF16), wo_ref[...])


def _merge(a2, b2, proj_a2, x2, mod_l, wa, wb, wo, seq):
    M, D = x2.shape
    tm = 256
    per_b = seq // tm
    B = mod_l.shape[1]
    mod4 = mod_l.reshape(3, B, 1, D)
    dcols = D // LANES
    full = lambda a: pl.BlockSpec(a.shape, lambda i: (0, 0))
    return pl.pallas_call(
        _merge_kernel,
        grid=(M // tm,),
        in_specs=[
            pl.BlockSpec((tm, a2.shape[1]), lambda i: (i, 0)),
            pl.BlockSpec((tm, b2.shape[1]), lambda i: (i, 0)),
            pl.BlockSpec((tm, D), lambda i: (i, CB_MA // dcols)),
            pl.BlockSpec((tm, D), lambda i: (i, CB_MB // dcols)),
            pl.BlockSpec((tm, D), lambda i: (i, 0)),
            pl.BlockSpec((None, None, 1, D), lambda i: (2, i // per_b, 0, 0)),
            full(wa), full(wb), full(wo),
        ],
        out_specs=pl.BlockSpec((tm, D), lambda i: (i, 0)),
        out_shape=jax.ShapeDtypeStruct((M, D), F32),
        compiler_params=pltpu.CompilerParams(
            dimension_semantics=("parallel",), vmem_limit_bytes=VMEM_LIMIT),
        name="merge_out",
    )(a2, b2, proj_a2, proj_a2, x2, mod4, wa, wb, wo)


def _permute_w_in(w_in):
    sb, nw, kv = SB_HEADS * HEAD_DIM, NSA_HEADS * HEAD_DIM, NSA_KV_HEADS * HEAD_DIM
    D = w_in.shape[1]
    sizes = (sb, sb, sb, sb, nw, kv, kv, kv, kv, kv, kv, nw, 3 * NSA_HEADS, D, D)
    offs = np.concatenate([[0], np.cumsum(sizes)])
    assert offs[-1] == w_in.shape[2]
    seg = lambda k: w_in[:, :, offs[k]:offs[k + 1]]
    pad = lambda n: jnp.zeros(w_in.shape[:2] + (n,), w_in.dtype)
    part_a = jnp.concatenate([seg(0) * Q_SCALE2, seg(1), seg(2), seg(3), seg(13), seg(14), seg(11),
                              seg(6), seg(8), seg(10), pad(LANES)], axis=-1)
    part_b = jnp.concatenate([seg(4), seg(5), seg(7), seg(9), seg(12), pad(LANES - 3 * NSA_HEADS)], axis=-1)
    assert part_a.shape[-1] == N_PROJ_A and part_b.shape[-1] == N_PROJ_B
    return jnp.concatenate([part_a, part_b], axis=-1).astype(BF16)


def _rope_tables(pos):
    half = HEAD_DIM // 2
    freq = ROPE_THETA ** (-jnp.arange(half, dtype=F32) / half)
    ang = pos.astype(F32)[:, None] * freq[None, :]
    cos, sin = jnp.cos(ang), jnp.sin(ang)
    reps = LANES // HEAD_DIM
    return jnp.tile(jnp.concatenate([cos, cos], -1), (1, reps)), jnp.tile(jnp.concatenate([-sin, sin], -1), (1, reps))


def kernel(x, c, ada_w, ada_b, norm_g, w_in, q_norm_g, kc_norm_g, ks_norm_g, kw_norm_g, cmp_pe_k, cmp_w1_k, cmp_w2_k, cmp_pe_v, cmp_w1_v, cmp_w2_v, w_proj_a, w_proj_b, w_out):
    B, S, D = x.shape
    L = ada_w.shape[0]
    M = B * S
    assert D == 8 * LANES and S % 1024 == 0 and S // SLC_BLOCK >= SLC_TOPK and S >= WINDOW + TQ
    assert TQ == LANES and S // SLC_BLOCK <= LANES
    n_cmp_rows = S // CMP_STRIDE

    mod = _modulation(c, ada_w, ada_b)
    w_p = _permute_w_in(w_in)
    wa, wb, wo = w_proj_a.astype(BF16), w_proj_b.astype(BF16), w_out.astype(BF16)
    cos, sin = _rope_tables(jnp.arange(S))
    cos_c, sin_c = _rope_tables(CMP_STRIDE * jnp.arange(n_cmp_rows) + CMP_LEN - 1)
    tile2 = lambda g: jnp.tile(g, (1, LANES // HEAD_DIM)).reshape(L, 1, LANES)
    q_g, kc_g, ks_g, kw_g = tile2(q_norm_g), tile2(kc_norm_g), tile2(ks_norm_g), tile2(kw_norm_g)

    x2 = x.reshape(M, D)
    for l in range(L):
        proj_a2, proj_b2 = _in_projection(x2, mod[l], norm_g[l], w_p[l], S)
        proj_a = proj_a2.reshape(B, S, N_PROJ_A)
        proj_b = proj_b2.reshape(B, S, N_PROJ_B)
        a = _sb_attention(proj_a)
        qn, ks, vs, kw, vw = _nsa_prep(proj_a, proj_b, cos, sin, q_g[l], ks_g[l], kw_g[l])
        xk = proj_b[:, :, CB_KC * LANES:(CB_KC + 1) * LANES].reshape(B, n_cmp_rows, CMP_STRIDE * LANES)
        xv = proj_a[:, :, CB_VC * LANES:(CB_VC + 1) * LANES].reshape(B, n_cmp_rows, CMP_STRIDE * LANES)
        kc, vc = _compress(xk, xv,
                           _expand_cmp_weights(cmp_pe_k[l], cmp_w1_k[l], cmp_w2_k[l]),
                           _expand_cmp_weights(cmp_pe_v[l], cmp_w1_v[l], cmp_w2_v[l]),
                           cos_c, sin_c, kc_g[l])
        b = _nsa_core(qn, kc, vc, ks, vs, kw, vw, proj_a, proj_b)
        x2 = _merge(a.reshape(M, -1), b.reshape(M, -1), proj_a2, x2, mod[l], wa[l], wb[l], wo[l], S)
    return x2.reshape(B, S, D)
```

```python
import functools
import math

import jax
import jax.numpy as jnp
import numpy as np
from jax import lax
from jax.experimental import pallas as pl
from jax.experimental.pallas import tpu as pltpu

HEAD_DIM = 64
LANES = 128
SB_HEADS = 8
NSA_HEADS = 8
NSA_KV_HEADS = 2
NSA_GROUP = NSA_HEADS // NSA_KV_HEADS
CMP_LEN = 32
CMP_STRIDE = 16
CMP_HIDDEN = 128
SLC_BLOCK = 64
SLC_TOPK = 8
WINDOW = 256
ROPE_THETA = 10000.0
EPS = 1e-6
NEG = -1e30
FORCE = 1e4
Q_SCALE2 = HEAD_DIM ** -0.5 * math.log2(math.e)
BIG = 2.0 ** 100

F32 = jnp.float32
BF16 = jnp.bfloat16

CB_SBQ, CB_SBK, CB_SBV, CB_SBZ = 0, 4, 8, 12
CB_MA, CB_MB, CB_NZ, CB_VC, CB_VS, CB_VW = 16, 24, 32, 36, 37, 38
N_PROJ_A = 40 * LANES
CB_NQ, CB_KC, CB_KS, CB_KW, CB_NG = 0, 4, 5, 6, 7
N_PROJ_B = 8 * LANES

TQ = 128
SLC_TK = 1024
SB_T = 256
SB_PAIRS = 4
VMEM_LIMIT = 56 * 1024 * 1024


def _dot(a, b):
    return jnp.dot(a, b, preferred_element_type=F32)


def _dot_nt(a, b):
    return lax.dot_general(a, b, (((1,), (1,)), ((), ())), preferred_element_type=F32)


def _split(x):
    hi = x.astype(BF16)
    lo = (x - hi.astype(F32)).astype(BF16)
    return hi, lo


def _dot_hl(x, m):
    hi, lo = _split(x)
    return _dot(hi, m) + _dot(lo, m)


def _dot3(a, b):
    ah, al = _split(a)
    bh, bl = _split(b)
    return _dot(ah, bh) + _dot(al, bh) + _dot(ah, bl)


def _sigmoid(x):
    return 1.0 / (1.0 + jnp.exp(-x))


def _silu(x):
    return x * _sigmoid(x)


def _iota(shape, dim):
    return lax.broadcasted_iota(jnp.int32, shape, dim)


def _head_block_ones():
    r = _iota((LANES, LANES), 0) // HEAD_DIM
    c = _iota((LANES, LANES), 1) // HEAD_DIM
    return (r == c).astype(BF16)


def _norm_rope(x, g, cos, sin_signed):
    ss = _dot_hl(x * x, _head_block_ones())
    y = x * lax.rsqrt(ss * (1.0 / HEAD_DIM) + EPS) * g
    lane = _iota(y.shape, 1)
    first_half = (lane % HEAD_DIM) < (HEAD_DIM // 2)
    swapped = jnp.where(first_half, pltpu.roll(y, LANES - HEAD_DIM // 2, 1),
                        pltpu.roll(y, HEAD_DIM // 2, 1))
    return y * cos + swapped * sin_signed


def _dup_halves(x):
    lane = _iota(x.shape, 1)
    r = pltpu.roll(x, HEAD_DIM, 1)
    low = lane < HEAD_DIM
    return jnp.where(low, x, r), jnp.where(low, r, x)


def _mod_kernel(c_ref, w_ref, b_ref, o_ref):
    o_ref[...] = _dot3(_silu(c_ref[...]), w_ref[...]) + b_ref[...]


def _modulation(c, ada_w, ada_b):
    L, D, _ = ada_w.shape
    B = c.shape[0]
    return pl.pallas_call(
        _mod_kernel,
        grid=(L, 3),
        in_specs=[
            pl.BlockSpec((B, D), lambda l, j: (0, 0)),
            pl.BlockSpec((None, D, D), lambda l, j: (l, 0, j)),
            pl.BlockSpec((None, 1, D), lambda l, j: (l, 0, j)),
        ],
        out_specs=pl.BlockSpec((None, None, B, D), lambda l, j: (l, j, 0, 0)),
        out_shape=jax.ShapeDtypeStruct((L, 3, B, D), F32),
        compiler_params=pltpu.CompilerParams(dimension_semantics=("arbitrary", "arbitrary")),
        name="adaln_mod",
    )(c, ada_w, ada_b.reshape(L, 1, 3 * D))


def _inproj_kernel(x_ref, shift_ref, scale_ref, g_ref, w_ref, oa_ref, ob_ref, h_ref):
    j = pl.program_id(1)
    n_a = pl.num_programs(1) - 1

    @pl.when(j == 0)
    def _():
        x = x_ref[...]
        ms = jnp.mean(x * x, axis=-1, keepdims=True)
        y = x * lax.rsqrt(ms + EPS) * g_ref[...]
        h_ref[...] = (y * (1.0 + scale_ref[...]) + shift_ref[...]).astype(BF16)

    @pl.when(j < n_a)
    def _():
        oa_ref[...] = _dot(h_ref[...], w_ref[...]).astype(BF16)

    @pl.when(j == n_a)
    def _():
        ob_ref[...] = _dot(h_ref[...], w_ref[...])


def _in_projection(x2, mod_l, norm_g, w_l, seq):
    M, D = x2.shape
    tm, tn = 1024, N_PROJ_B
    n_a = N_PROJ_A // tn
    per_b = seq // tm
    B = mod_l.shape[1]
    mod4 = mod_l.reshape(3, B, 1, D)
    return pl.pallas_call(
        _inproj_kernel,
        grid=(M // tm, n_a + 1),
        in_specs=[
            pl.BlockSpec((tm, D), lambda i, j: (i, 0)),
            pl.BlockSpec((None, None, 1, D), lambda i, j: (0, i // per_b, 0, 0)),
            pl.BlockSpec((None, None, 1, D), lambda i, j: (1, i // per_b, 0, 0)),
            pl.BlockSpec((1, D), lambda i, j: (0, 0)),
            pl.BlockSpec((D, tn), lambda i, j: (0, j)),
        ],
        out_specs=[pl.BlockSpec((tm, tn), lambda i, j: (i, jnp.minimum(j, n_a - 1))),
                   pl.BlockSpec((tm, tn), lambda i, j: (i, 0))],
        out_shape=[jax.ShapeDtypeStruct((M, N_PROJ_A), BF16), jax.ShapeDtypeStruct((M, N_PROJ_B), F32)],
        scratch_shapes=[pltpu.VMEM((tm, D), BF16)],
        compiler_params=pltpu.CompilerParams(
            dimension_semantics=("parallel", "arbitrary"), vmem_limit_bytes=VMEM_LIMIT),
        name="in_projection",
    )(x2, mod4, mod4, norm_g.reshape(1, D), w_l)


def _sb_kernel(q_ref, k_ref, v_ref, z_ref, o_ref):
    T = SB_T
    i = pl.program_id(2)
    n_pair = q_ref.shape[1] // LANES
    lane = _iota((T, LANES), 1)
    low = lane < HEAD_DIM
    q_heads = []
    for p in range(n_pair):
        q2 = q_ref[:, p * LANES:(p + 1) * LANES]
        zero = jnp.zeros_like(q2)
        q_heads += [jnp.where(low, q2, zero), jnp.where(low, zero, q2)]

    suffix = (_iota((T, T), 0) > _iota((T, T), 1)).astype(BF16)
    strict = _iota((T, T), 1) < _iota((T, T), 0)

    def chain(qh, k, v, acc, carry, diagonal):
        z = _dot_nt(qh, k)
        nz = -z
        log_keep = jnp.minimum(nz, 0.0) - jnp.log2(1.0 + jnp.exp2(jnp.minimum(z, nz)))
        log_hit = z + log_keep
        if diagonal:
            log_keep = jnp.where(strict, log_keep, 0.0)
        keep_b = log_keep.astype(BF16)
        within = _dot(keep_b, suffix)
        w = jnp.exp2(log_hit + within + carry)
        if diagonal:
            w = jnp.where(strict, w, 0.0)
        row_total = within[:, 0:1] + keep_b[:, 0:1].astype(F32)
        return acc + _dot(w.astype(BF16), v), carry + row_total

    def block(kb, state, diagonal):
        rows = pl.ds(pl.multiple_of(kb * T, T), T)
        out = []
        for h in range(2 * n_pair):
            cols = slice((h // 2) * LANES, (h // 2 + 1) * LANES)
            out += chain(q_heads[h], k_ref[rows, cols], v_ref[rows, cols], state[2 * h], state[2 * h + 1], diagonal)
        return tuple(out)

    acc0 = jnp.zeros((T, LANES), F32)
    carry0 = jnp.zeros((T, 1), F32)
    state = block(i, (acc0, carry0) * (2 * n_pair), True)
    state = lax.fori_loop(0, i, lambda jj, st: block(i - 1 - jj, st, False), state)
    for p in range(n_pair):
        out = jnp.where(low, state[4 * p], state[4 * p + 2])
        cols = slice(p * LANES, (p + 1) * LANES)
        o_ref[:, cols] = (out * _silu(z_ref[:, cols].astype(F32))).astype(BF16)


def _sb_attention(proj_a):
    B, S, _ = proj_a.shape
    pairs = SB_HEADS // 2
    T = SB_T
    pp = SB_PAIRS
    w = pp * LANES
    return pl.pallas_call(
        _sb_kernel,
        grid=(B, pairs // pp, S // T),
        in_specs=[
            pl.BlockSpec((None, T, w), lambda b, p, i: (b, i, CB_SBQ // pp + p)),
            pl.BlockSpec((None, S, w), lambda b, p, i: (b, 0, CB_SBK // pp + p)),
            pl.BlockSpec((None, S, w), lambda b, p, i: (b, 0, CB_SBV // pp + p)),
            pl.BlockSpec((None, T, w), lambda b, p, i: (b, i, CB_SBZ // pp + p)),
        ],
        out_specs=pl.BlockSpec((None, T, w), lambda b, p, i: (b, i, p)),
        out_shape=jax.ShapeDtypeStruct((B, S, pairs * LANES), BF16),
        compiler_params=pltpu.CompilerParams(
            dimension_semantics=("parallel", "parallel", "arbitrary"), vmem_limit_bytes=VMEM_LIMIT),
        name="stick_breaking",
    )(proj_a, proj_a, proj_a, proj_a)


def _prep_kernel(q_ref, ks_ref, vs_ref, kw_ref, vw_ref, cos_ref, sin_ref, qg_ref, ksg_ref, kwg_ref,
                 qo_ref, kso_ref, vso_ref, kwo_ref, vwo_ref):
    tm = q_ref.shape[0]
    cos = cos_ref[...]
    sin = sin_ref[...]
    for p in range(NSA_HEADS // 2):
        x = q_ref[:, p * LANES:(p + 1) * LANES]
        qo_ref[:, p * LANES:(p + 1) * LANES] = (
            _norm_rope(x, qg_ref[...], cos, sin) * Q_SCALE2).astype(BF16)
    pos = pl.program_id(1) * tm + _iota((tm, LANES), 0)
    own_block = jnp.where(_iota((tm, LANES), 1) == pos // SLC_BLOCK, -BIG, 0.0).astype(BF16)
    ones = jnp.ones((tm, LANES), BF16)
    for g, half in enumerate(_dup_halves(_norm_rope(ks_ref[...], ksg_ref[...], cos, sin))):
        kso_ref[g] = jnp.concatenate([half.astype(BF16), own_block], axis=1)
    for g, half in enumerate(_dup_halves(_norm_rope(kw_ref[...], kwg_ref[...], cos, sin))):
        kwo_ref[g] = half.astype(BF16)
    for src, dst in ((vs_ref, vso_ref), (vw_ref, vwo_ref)):
        for g, half in enumerate(_dup_halves(src[...].astype(F32))):
            dst[g] = jnp.concatenate([half.astype(BF16), ones], axis=1)


def _nsa_prep(proj_a, proj_b, cos, sin, q_g, ks_g, kw_g):
    B, S, _ = proj_b.shape
    tm = 256
    G = NSA_KV_HEADS
    col = lambda cb, w: pl.BlockSpec((None, tm, w * LANES), lambda b, i: (b, i, cb // w))
    tab = pl.BlockSpec((tm, LANES), lambda b, i: (i, 0))
    gain = pl.BlockSpec((1, LANES), lambda b, i: (0, 0))
    kv_out = lambda w: pl.BlockSpec((None, G, tm, w * LANES), lambda b, i: (b, 0, i, 0))
    kv_shape = lambda w: jax.ShapeDtypeStruct((B, G, S, w * LANES), BF16)
    return pl.pallas_call(
        _prep_kernel,
        grid=(B, S // tm),
        in_specs=[col(CB_NQ, 4), col(CB_KS, 1), col(CB_VS, 1), col(CB_KW, 1), col(CB_VW, 1),
                  tab, tab, gain, gain, gain],
        out_specs=[pl.BlockSpec((None, tm, 4 * LANES), lambda b, i: (b, i, 0)),
                   kv_out(2), kv_out(2), kv_out(1), kv_out(2)],
        out_shape=[jax.ShapeDtypeStruct((B, S, 4 * LANES), BF16),
                   kv_shape(2), kv_shape(2), kv_shape(1), kv_shape(2)],
        compiler_params=pltpu.CompilerParams(
            dimension_semantics=("parallel", "parallel"), vmem_limit_bytes=VMEM_LIMIT),
        name="nsa_prep",
    )(proj_b, proj_b, proj_a, proj_b, proj_a, cos, sin, q_g, ks_g, kw_g)


def _compress_kernel(xk_ref, xv_ref, pek_ref, pev_ref, w1k_ref, w1v_ref, w2k_ref, w2v_ref,
                     cos_ref, sin_ref, g_ref, ko_ref, vo_ref):
    nrow = xk_ref.shape[0]
    half = CMP_STRIDE * LANES

    def mlp(x_ref, pe_ref, w1_ref, w2_ref):
        x = x_ref[...]
        first = _dot((x + pe_ref[0:1, :]).astype(BF16), w1_ref[0:half, :])
        second = _dot((x + pe_ref[1:2, :]).astype(BF16), w1_ref[half:2 * half, :])
        pre = first + pltpu.roll(second, nrow - 1, 0)
        return _dot(_silu(pre).astype(BF16), w2_ref[...])

    kc = _norm_rope(mlp(xk_ref, pek_ref, w1k_ref, w2k_ref), g_ref[...], cos_ref[...], sin_ref[...])
    a, b = _dup_halves(kc)
    ko_ref[0] = a.astype(BF16)
    ko_ref[1] = b.astype(BF16)
    a, b = _dup_halves(mlp(xv_ref, pev_ref, w1v_ref, w2v_ref))
    vo_ref[0] = a.astype(BF16)
    vo_ref[1] = b.astype(BF16)


def _expand_cmp_weights(pe, w1, w2):
    G = NSA_KV_HEADS
    eye = jnp.eye(G, dtype=w1.dtype)
    w1h = w1.reshape(2, CMP_STRIDE, HEAD_DIM, CMP_HIDDEN)
    w1e = jnp.einsum('aldh,gk->algdkh', w1h, eye).reshape(2 * CMP_STRIDE * G * HEAD_DIM, G * CMP_HIDDEN)
    w2e = jnp.einsum('hd,gk->ghkd', w2, eye).reshape(G * CMP_HIDDEN, G * HEAD_DIM)
    pee = jnp.broadcast_to(pe.reshape(2, CMP_STRIDE, 1, HEAD_DIM), (2, CMP_STRIDE, G, HEAD_DIM))
    return pee.reshape(2, CMP_STRIDE * G * HEAD_DIM), w1e.astype(BF16), w2e.astype(BF16)


def _compress(xk, xv, wk, wv, cos_c, sin_c, kc_g):
    B, nrow, width = xk.shape
    G = NSA_KV_HEADS
    full = lambda a: pl.BlockSpec(a.shape, lambda b: (0,) * a.ndim)
    xspec = pl.BlockSpec((None, nrow, width), lambda b: (b, 0, 0))
    ospec = pl.BlockSpec((None, G, nrow, LANES), lambda b: (b, 0, 0, 0))
    oshape = jax.ShapeDtypeStruct((B, G, nrow, LANES), BF16)
    args = (xk, xv, wk[0], wv[0], wk[1], wv[1], wk[2], wv[2], cos_c, sin_c, kc_g)
    return pl.pallas_call(
        _compress_kernel,
        grid=(B,),
        in_specs=[xspec, xspec] + [full(a) for a in args[2:]],
        out_specs=[ospec, ospec],
        out_shape=[oshape, oshape],
        compiler_params=pltpu.CompilerParams(
            dimension_semantics=("parallel",), vmem_limit_bytes=VMEM_LIMIT),
        name="nsa_compress",
    )(*args)


def _nsa_kernel(q_ref, kc_ref, vc_ref, ks_ref, vs_ref, kw_ref, vw_ref, ng_ref, nz_ref, o_ref, s_ref, *, seq):
    i = pl.program_id(1)
    G = NSA_KV_HEADS
    R = NSA_GROUP * TQ
    n_pairs = NSA_GROUP // 2
    n_blocks = seq // SLC_BLOCK
    n_cmp = (seq - CMP_LEN) // CMP_STRIDE + 1
    nc = kc_ref.shape[1]
    t0 = i * TQ

    lane_q = _iota((TQ, LANES), 1)
    low = lane_q < HEAD_DIM
    zero = jnp.zeros((TQ, LANES), BF16)
    onehot_t = (_iota((R, LANES), 1) == _iota((R, LANES), 0) % TQ).astype(BF16)

    def masked_bias(allowed):
        return jnp.where(allowed, 0.0, -BIG).astype(BF16)

    n_i = _iota((nc, LANES), 0)
    t_i = t0 + _iota((nc, LANES), 1)
    bias_c = masked_bias((CMP_STRIDE * n_i + (CMP_LEN - 1) <= t_i) & (n_i < n_cmp))
    has_valid = t0 + _iota((R, nc), 0) % TQ >= CMP_LEN - 1

    span = WINDOW + TQ
    start = pl.multiple_of(jnp.clip(t0 - WINDOW, 0, seq - span), TQ)
    k_pos = start + _iota((span, LANES), 0)
    t_w = t0 + _iota((span, LANES), 1)
    bias_w = masked_bias((k_pos <= t_w) & (k_pos > t_w - WINDOW))

    jn = _iota((LANES, nc), 0)
    nn = _iota((LANES, nc), 1)
    overlap_t = ((CMP_STRIDE * nn < SLC_BLOCK * (jn + 1)) & (CMP_STRIDE * nn + CMP_LEN > SLC_BLOCK * jn)
                 & (nn < n_cmp) & (jn < n_blocks)).astype(BF16)
    nb8 = -(-n_blocks // 8) * 8
    jb = _iota((nb8, TQ), 0)
    cur = (t0 + _iota((nb8, TQ), 1)) // SLC_BLOCK
    eye = (_iota((TQ, TQ), 0) == _iota((TQ, TQ), 1)).astype(BF16)

    o_cmp_g, o_win_g, qs_sel_g = [], [], []
    for g in range(G):
        slabs = []
        for p in range(n_pairs):
            qp = q_ref[:, (g * n_pairs + p) * LANES:(g * n_pairs + p + 1) * LANES]
            slabs += [jnp.where(low, qp, zero), jnp.where(low, zero, qp)]
        qs = jnp.concatenate(slabs, axis=0)
        qs_pos = jnp.concatenate([qs, onehot_t], axis=1)

        s_c = _dot_nt(qs_pos, jnp.concatenate([kc_ref[g], bias_c], axis=1))
        e = jnp.exp2(s_c - jnp.max(s_c, axis=-1, keepdims=True))
        p_c = jnp.where(has_valid, e / jnp.sum(e, axis=-1, keepdims=True), 0.0)
        o_cmp_g.append(_dot(p_c.astype(BF16), vc_ref[g]))

        s_w = _dot_nt(qs_pos, jnp.concatenate([kw_ref[g, pl.ds(start, span), :], bias_w], axis=1))
        ew = jnp.exp2(s_w - jnp.max(s_w, axis=-1, keepdims=True)).astype(BF16)
        ow = _dot(ew, vw_ref[g, pl.ds(start, span), :])
        o_win_g.append(ow[:, :LANES] / ow[:, LANES:])

        p_sum = p_c[0:TQ]
        for h in range(1, NSA_GROUP):
            p_sum = p_sum + p_c[h * TQ:(h + 1) * TQ]
        p_hi, p_lo = _split(p_sum)
        imp = (_dot_nt(overlap_t, p_hi) + _dot_nt(overlap_t, p_lo))[:nb8]
        imp = jnp.where(jb > cur, NEG, imp)
        imp = jnp.where((jb == 0) | (jb == cur - 1), FORCE, imp)
        imp = jnp.where(jb == cur, 2.0 * FORCE, imp)
        rank = jnp.zeros((nb8, TQ), jnp.int32)
        for j in range(n_blocks):
            row = jnp.broadcast_to(imp[j:j + 1, :], (nb8, TQ))
            ahead = (row > imp) | ((row == imp) & (jb > j))
            rank = rank + ahead.astype(jnp.int32)
        chosen_t = (rank < min(SLC_TOPK, n_blocks)) & (jb <= cur) & (jb < n_blocks)
        not_t = jnp.where(chosen_t, 0.0, 1.0)
        if nb8 < LANES:
            not_t = jnp.concatenate([not_t, jnp.zeros((LANES - nb8, TQ), F32)], axis=0)
        not_chosen = _dot_nt(eye, not_t.astype(BF16)).astype(BF16)
        qs_sel_g.append(jnp.concatenate([qs, jnp.concatenate([not_chosen] * NSA_GROUP, axis=0)], axis=1))

    last = (t0 + TQ - 1) // SLC_TK

    def chunk(ref, g, c):
        return ref[g, pl.ds(pl.multiple_of(c * SLC_TK, SLC_TK), SLC_TK), :]

    def fold(s):
        out = s[:, :LANES]
        for k in range(1, SLC_TK // LANES):
            out = jnp.maximum(out, s[:, k * LANES:(k + 1) * LANES])
        return out

    def pass1(c, run):
        out = []
        for g in range(G):
            s = _dot_nt(qs_sel_g[g], chunk(ks_ref, g, c))
            s_ref[g, c] = s
            out.append(jnp.maximum(run[g], fold(s)))
        return tuple(out)

    run = lax.fori_loop(0, last, pass1, (jnp.full((R, LANES), -3.0e38, F32),) * G)
    causal = last * SLC_TK + _iota((R, SLC_TK), 1) <= t0 + _iota((R, SLC_TK), 0) % TQ
    m_sel = []
    for g in range(G):
        s = jnp.where(causal, _dot_nt(qs_sel_g[g], chunk(ks_ref, g, last)), -BIG)
        s_ref[g, last] = s
        m_sel.append(jnp.broadcast_to(jnp.max(jnp.maximum(run[g], fold(s)), axis=-1, keepdims=True), (R, SLC_TK)))

    def pass2(c, acc):
        out = []
        for g in range(G):
            p = jnp.exp2(s_ref[g, c] - m_sel[g]).astype(BF16)
            out.append(acc[g] + _dot(p, chunk(vs_ref, g, c)))
        return tuple(out)

    acc = lax.fori_loop(0, last + 1, pass2, (jnp.zeros((R, 2 * LANES), F32),) * G)

    sig_hi, sig_lo = _split(_sigmoid(ng_ref[...]))
    width = 3 * n_pairs * LANES
    col = _iota((LANES, width), 1)
    for g in range(G):
        o_slc = acc[g][:, :LANES] / acc[g][:, LANES:]
        want = (col // (n_pairs * LANES)) * NSA_HEADS + g * NSA_GROUP + 2 * ((col // LANES) % n_pairs) \
            + (col % LANES) // HEAD_DIM
        pick = (_iota((LANES, width), 0) == want).astype(BF16)
        gates = _dot(sig_hi, pick) + _dot(sig_lo, pick)
        for p in range(n_pairs):
            total = None
            for br, o in enumerate((o_cmp_g[g], o_slc, o_win_g[g])):
                pair = jnp.where(low, o[2 * p * TQ:(2 * p + 1) * TQ], o[(2 * p + 1) * TQ:(2 * p + 2) * TQ])
                c0 = (br * n_pairs + p) * LANES
                term = gates[:, c0:c0 + LANES] * pair
                total = term if total is None else total + term
            cols = slice((g * n_pairs + p) * LANES, (g * n_pairs + p + 1) * LANES)
            o_ref[:, cols] = (total * _silu(nz_ref[:, cols].astype(F32))).astype(BF16)


def _nsa_core(qn, kc, vc, ks, vs, kw, vw, proj_a, proj_b):
    B, S, _ = proj_b.shape
    G = NSA_KV_HEADS
    width = NSA_HEADS * HEAD_DIM
    ncmp = kc.shape[2]
    kvc = pl.BlockSpec((None, G, ncmp, LANES), lambda b, i: (b, 0, 0, 0))
    kvs = lambda w: pl.BlockSpec((None, G, S, w * LANES), lambda b, i: (b, 0, 0, 0))
    return pl.pallas_call(
        functools.partial(_nsa_kernel, seq=S),
        grid=(B, S // TQ),
        in_specs=[
            pl.BlockSpec((None, TQ, width), lambda b, i: (b, i, 0)),
            kvc, kvc, kvs(2), kvs(2), kvs(1), kvs(2),
            pl.BlockSpec((None, TQ, LANES), lambda b, i: (b, i, CB_NG)),
            pl.BlockSpec((None, TQ, width), lambda b, i: (b, i, CB_NZ * LANES // width)),
        ],
        out_specs=pl.BlockSpec((None, TQ, width), lambda b, i: (b, i, 0)),
        out_shape=jax.ShapeDtypeStruct((B, S, width), BF16),
        scratch_shapes=[pltpu.VMEM((G, S // SLC_TK, NSA_GROUP * TQ, SLC_TK), F32)],
        compiler_params=pltpu.CompilerParams(
            dimension_semantics=("parallel", "arbitrary"), vmem_limit_bytes=VMEM_LIMIT),
        name="nsa_core",
    )(qn, kc, vc, ks, vs, kw, vw, proj_b, proj_a)


def _merge_kernel(a_ref, b_ref, ma_ref, mb_ref, x_ref, gate_ref, wa_ref, wb_ref, wo_ref, o_ref):
    ya = _dot(a_ref[...], wa_ref[...])
    yb = _dot(b_ref[...], wb_ref[...])
    y = _sigmoid(ma_ref[...].astype(F32)) * ya + _sigmoid(mb_ref[...].astype(F32)) * yb
    o_ref[...] = x_ref[...] + gate_ref[...] * _dot(y.astype(BF16), wo_ref[...])


def _merge(a2, b2, proj_a2, x2, mod_l, wa, wb, wo, seq):
    M, D = x2.shape
    tm = 256
    per_b = seq // tm
    B = mod_l.shape[1]
    mod4 = mod_l.reshape(3, B, 1, D)
    dcols = D // LANES
    full = lambda a: pl.BlockSpec(a.shape, lambda i: (0, 0))
    return pl.pallas_call(
        _merge_kernel,
        grid=(M // tm,),
        in_specs=[
            pl.BlockSpec((tm, a2.shape[1]), lambda i: (i, 0)),
            pl.BlockSpec((tm, b2.shape[1]), lambda i: (i, 0)),
            pl.BlockSpec((tm, D), lambda i: (i, CB_MA // dcols)),
            pl.BlockSpec((tm, D), lambda i: (i, CB_MB // dcols)),
            pl.BlockSpec((tm, D), lambda i: (i, 0)),
            pl.BlockSpec((None, None, 1, D), lambda i: (2, i // per_b, 0, 0)),
            full(wa), full(wb), full(wo),
        ],
        out_specs=pl.BlockSpec((tm, D), lambda i: (i, 0)),
        out_shape=jax.ShapeDtypeStruct((M, D), F32),
        compiler_params=pltpu.CompilerParams(
            dimension_semantics=("parallel",), vmem_limit_bytes=VMEM_LIMIT),
        name="merge_out",
    )(a2, b2, proj_a2, proj_a2, x2, mod4, wa, wb, wo)


def _permute_w_in(w_in):
    sb, nw, kv = SB_HEADS * HEAD_DIM, NSA_HEADS * HEAD_DIM, NSA_KV_HEADS * HEAD_DIM
    D = w_in.shape[1]
    sizes = (sb, sb, sb, sb, nw, kv, kv, kv, kv, kv, kv, nw, 3 * NSA_HEADS, D, D)
    offs = np.concatenate([[0], np.cumsum(sizes)])
    assert offs[-1] == w_in.shape[2]
    w_b = w_in.astype(BF16)
    seg = lambda k: w_b[:, :, offs[k]:offs[k + 1]]
    pad = lambda n: jnp.zeros(w_in.shape[:2] + (n,), BF16)
    sb_q = (w_in[:, :, offs[0]:offs[1]] * Q_SCALE2).astype(BF16)
    part_a = jnp.concatenate([sb_q, seg(1), seg(2), seg(3), seg(13), seg(14), seg(11),
                              seg(6), seg(8), seg(10), pad(LANES)], axis=-1)
    part_b = jnp.concatenate([seg(4), seg(5), seg(7), seg(9), seg(12), pad(LANES - 3 * NSA_HEADS)], axis=-1)
    assert part_a.shape[-1] == N_PROJ_A and part_b.shape[-1] == N_PROJ_B
    return jnp.concatenate([part_a, part_b], axis=-1)


def _rope_tables(pos):
    half = HEAD_DIM // 2
    freq = ROPE_THETA ** (-jnp.arange(half, dtype=F32) / half)
    ang = pos.astype(F32)[:, None] * freq[None, :]
    cos, sin = jnp.cos(ang), jnp.sin(ang)
    reps = LANES // HEAD_DIM
    return jnp.tile(jnp.concatenate([cos, cos], -1), (1, reps)), jnp.tile(jnp.concatenate([-sin, sin], -1), (1, reps))


def kernel(x, c, ada_w, ada_b, norm_g, w_in, q_norm_g, kc_norm_g, ks_norm_g, kw_norm_g, cmp_pe_k, cmp_w1_k, cmp_w2_k, cmp_pe_v, cmp_w1_v, cmp_w2_v, w_proj_a, w_proj_b, w_out):
    B, S, D = x.shape
    L = ada_w.shape[0]
    M = B * S
    assert D == 8 * LANES and S % 1024 == 0 and S // SLC_BLOCK >= SLC_TOPK and S >= WINDOW + TQ
    assert TQ == LANES and S // SLC_BLOCK <= LANES and S % SLC_TK == 0 and S % SB_T == 0
    n_cmp_rows = S // CMP_STRIDE

    mod = _modulation(c, ada_w, ada_b)
    w_p = _permute_w_in(w_in)
    wa, wb, wo = w_proj_a.astype(BF16), w_proj_b.astype(BF16), w_out.astype(BF16)
    cos, sin = _rope_tables(jnp.arange(S))
    cos_c, sin_c = _rope_tables(CMP_STRIDE * jnp.arange(n_cmp_rows) + CMP_LEN - 1)
    tile2 = lambda g: jnp.tile(g, (1, LANES // HEAD_DIM)).reshape(L, 1, LANES)
    q_g, kc_g, ks_g, kw_g = tile2(q_norm_g), tile2(kc_norm_g), tile2(ks_norm_g), tile2(kw_norm_g)

    x2 = x.reshape(M, D)
    for l in range(L):
        proj_a2, proj_b2 = _in_projection(x2, mod[l], norm_g[l], w_p[l], S)
        proj_a = proj_a2.reshape(B, S, N_PROJ_A)
        proj_b = proj_b2.reshape(B, S, N_PROJ_B)
        a = _sb_attention(proj_a)
        qn, ks, vs, kw, vw = _nsa_prep(proj_a, proj_b, cos, sin, q_g[l], ks_g[l], kw_g[l])
        xk = proj_b[:, :, CB_KC * LANES:(CB_KC + 1) * LANES].reshape(B, n_cmp_rows, CMP_STRIDE * LANES)
        xv = proj_a[:, :, CB_VC * LANES:(CB_VC + 1) * LANES].reshape(B, n_cmp_rows, CMP_STRIDE * LANES)
        kc, vc = _compress(xk, xv,
                           _expand_cmp_weights(cmp_pe_k[l], cmp_w1_k[l], cmp_w2_k[l]),
                           _expand_cmp_weights(cmp_pe_v[l], cmp_w1_v[l], cmp_w2_v[l]),
                           cos_c, sin_c, kc_g[l])
        b = _nsa_core(qn, kc, vc, ks, vs, kw, vw, proj_a, proj_b)
        x2 = _merge(a.reshape(M, -1), b.reshape(M, -1), proj_a2, x2, mod[l], wa[l], wb[l], wo[l], S)
    return x2.reshape(B, S, D)
```

```python
import functools
import math

import jax
import jax.numpy as jnp
import numpy as np
from jax import lax
from jax.experimental import pallas as pl
from jax.experimental.pallas import tpu as pltpu

HEAD_DIM = 64
LANES = 128
SB_HEADS = 8
NSA_HEADS = 8
NSA_KV_HEADS = 2
NSA_GROUP = NSA_HEADS // NSA_KV_HEADS
CMP_LEN = 32
CMP_STRIDE = 16
CMP_HIDDEN = 128
SLC_BLOCK = 64
SLC_TOPK = 8
WINDOW = 256
ROPE_THETA = 10000.0
EPS = 1e-6
NEG = -1e30
FORCE = 1e4
Q_SCALE2 = HEAD_DIM ** -0.5 * math.log2(math.e)
BIG = 2.0 ** 100

F32 = jnp.float32
BF16 = jnp.bfloat16

CB_SBQ, CB_SBK, CB_SBV, CB_SBZ = 0, 4, 8, 12
CB_MA, CB_MB, CB_NZ, CB_VC, CB_VS, CB_VW = 16, 24, 32, 36, 37, 38
N_PROJ_A = 40 * LANES
CB_NQ, CB_KC, CB_KS, CB_KW, CB_NG = 0, 4, 5, 6, 7
N_PROJ_B = 8 * LANES

TQ = 128
SLC_TK = 1024
SB_T = 512
SB_PAIRS = 2
VMEM_LIMIT = 56 * 1024 * 1024


def _dot(a, b):
    return jnp.dot(a, b, preferred_element_type=F32)


def _dot_nt(a, b):
    return lax.dot_general(a, b, (((1,), (1,)), ((), ())), preferred_element_type=F32)


def _split(x):
    hi = x.astype(BF16)
    lo = (x - hi.astype(F32)).astype(BF16)
    return hi, lo


def _dot_hl(x, m):
    hi, lo = _split(x)
    return _dot(hi, m) + _dot(lo, m)


def _dot3(a, b):
    ah, al = _split(a)
    bh, bl = _split(b)
    return _dot(ah, bh) + _dot(al, bh) + _dot(ah, bl)


def _sigmoid(x):
    return 1.0 / (1.0 + jnp.exp(-x))


def _silu(x):
    return x * _sigmoid(x)


def _iota(shape, dim):
    return lax.broadcasted_iota(jnp.int32, shape, dim)


def _head_block_ones():
    r = _iota((LANES, LANES), 0) // HEAD_DIM
    c = _iota((LANES, LANES), 1) // HEAD_DIM
    return (r == c).astype(BF16)


def _norm_rope(x, g, cos, sin_signed):
    ss = _dot_hl(x * x, _head_block_ones())
    y = x * lax.rsqrt(ss * (1.0 / HEAD_DIM) + EPS) * g
    lane = _iota(y.shape, 1)
    first_half = (lane % HEAD_DIM) < (HEAD_DIM // 2)
    swapped = jnp.where(first_half, pltpu.roll(y, LANES - HEAD_DIM // 2, 1),
                        pltpu.roll(y, HEAD_DIM // 2, 1))
    return y * cos + swapped * sin_signed


def _dup_halves(x):
    lane = _iota(x.shape, 1)
    r = pltpu.roll(x, HEAD_DIM, 1)
    low = lane < HEAD_DIM
    return jnp.where(low, x, r), jnp.where(low, r, x)


def _mod_kernel(c_ref, w_ref, b_ref, o_ref):
    o_ref[...] = _dot3(_silu(c_ref[...]), w_ref[...]) + b_ref[...]


def _modulation(c, ada_w, ada_b):
    L, D, _ = ada_w.shape
    B = c.shape[0]
    return pl.pallas_call(
        _mod_kernel,
        grid=(L, 3),
        in_specs=[
            pl.BlockSpec((B, D), lambda l, j: (0, 0)),
            pl.BlockSpec((None, D, D), lambda l, j: (l, 0, j)),
            pl.BlockSpec((None, 1, D), lambda l, j: (l, 0, j)),
        ],
        out_specs=pl.BlockSpec((None, None, B, D), lambda l, j: (l, j, 0, 0)),
        out_shape=jax.ShapeDtypeStruct((L, 3, B, D), F32),
        compiler_params=pltpu.CompilerParams(dimension_semantics=("arbitrary", "arbitrary")),
        name="adaln_mod",
    )(c, ada_w, ada_b.reshape(L, 1, 3 * D))


def _inproj_kernel(x_ref, shift_ref, scale_ref, g_ref, w_ref, oa_ref, ob_ref, h_ref):
    j = pl.program_id(1)
    n_a = pl.num_programs(1) - 1

    @pl.when(j == 0)
    def _():
        x = x_ref[...]
        ms = jnp.mean(x * x, axis=-1, keepdims=True)
        y = x * lax.rsqrt(ms + EPS) * g_ref[...]
        h_ref[...] = (y * (1.0 + scale_ref[...]) + shift_ref[...]).astype(BF16)

    @pl.when(j < n_a)
    def _():
        oa_ref[...] = _dot(h_ref[...], w_ref[...]).astype(BF16)

    @pl.when(j == n_a)
    def _():
        ob_ref[...] = _dot(h_ref[...], w_ref[...])


def _in_projection(x2, mod_l, norm_g, w_l, seq):
    M, D = x2.shape
    tm, tn = 1024, N_PROJ_B
    n_a = N_PROJ_A // tn
    per_b = seq // tm
    B = mod_l.shape[1]
    mod4 = mod_l.reshape(3, B, 1, D)
    return pl.pallas_call(
        _inproj_kernel,
        grid=(M // tm, n_a + 1),
        in_specs=[
            pl.BlockSpec((tm, D), lambda i, j: (i, 0)),
            pl.BlockSpec((None, None, 1, D), lambda i, j: (0, i // per_b, 0, 0)),
            pl.BlockSpec((None, None, 1, D), lambda i, j: (1, i // per_b, 0, 0)),
            pl.BlockSpec((1, D), lambda i, j: (0, 0)),
            pl.BlockSpec((D, tn), lambda i, j: (0, j)),
        ],
        out_specs=[pl.BlockSpec((tm, tn), lambda i, j: (i, jnp.minimum(j, n_a - 1))),
                   pl.BlockSpec((tm, tn), lambda i, j: (i, 0))],
        out_shape=[jax.ShapeDtypeStruct((M, N_PROJ_A), BF16), jax.ShapeDtypeStruct((M, N_PROJ_B), F32)],
        scratch_shapes=[pltpu.VMEM((tm, D), BF16)],
        compiler_params=pltpu.CompilerParams(
            dimension_semantics=("parallel", "arbitrary"), vmem_limit_bytes=VMEM_LIMIT),
        name="in_projection",
    )(x2, mod4, mod4, norm_g.reshape(1, D), w_l)


def _sb_kernel(q_ref, k_ref, v_ref, z_ref, o_ref):
    T = SB_T
    i = pl.program_id(2)
    n_pair = q_ref.shape[1] // LANES
    lane = _iota((T, LANES), 1)
    low = lane < HEAD_DIM
    q_heads = []
    for p in range(n_pair):
        q2 = q_ref[:, p * LANES:(p + 1) * LANES]
        zero = jnp.zeros_like(q2)
        q_heads += [jnp.where(low, q2, zero), jnp.where(low, zero, q2)]

    suffix = (_iota((T, T), 0) > _iota((T, T), 1)).astype(BF16)
    strict = _iota((T, T), 1) < _iota((T, T), 0)

    def chain(qh, k, v, acc, carry, diagonal):
        z = _dot_nt(qh, k)
        nz = -z
        log_keep = jnp.minimum(nz, 0.0) - jnp.log2(1.0 + jnp.exp2(jnp.minimum(z, nz)))
        log_hit = z + log_keep
        if diagonal:
            log_keep = jnp.where(strict, log_keep, 0.0)
        keep_b = log_keep.astype(BF16)
        within = _dot(keep_b, suffix)
        w = jnp.exp2(log_hit + within + carry)
        if diagonal:
            w = jnp.where(strict, w, 0.0)
        row_total = within[:, 0:1] + keep_b[:, 0:1].astype(F32)
        return acc + _dot(w.astype(BF16), v), carry + row_total

    def block(kb, state, diagonal):
        rows = pl.ds(pl.multiple_of(kb * T, T), T)
        out = []
        for h in range(2 * n_pair):
            cols = slice((h // 2) * LANES, (h // 2 + 1) * LANES)
            out += chain(q_heads[h], k_ref[rows, cols], v_ref[rows, cols], state[2 * h], state[2 * h + 1], diagonal)
        return tuple(out)

    acc0 = jnp.zeros((T, LANES), F32)
    carry0 = jnp.zeros((T, 1), F32)
    state = block(i, (acc0, carry0) * (2 * n_pair), True)
    state = lax.fori_loop(0, i, lambda jj, st: block(i - 1 - jj, st, False), state)
    for p in range(n_pair):
        out = jnp.where(low, state[4 * p], state[4 * p + 2])
        cols = slice(p * LANES, (p + 1) * LANES)
        o_ref[:, cols] = (out * _silu(z_ref[:, cols].astype(F32))).astype(BF16)


def _sb_attention(proj_a):
    B, S, _ = proj_a.shape
    pairs = SB_HEADS // 2
    T = SB_T
    pp = SB_PAIRS
    w = pp * LANES
    return pl.pallas_call(
        _sb_kernel,
        grid=(B, pairs // pp, S // T),
        in_specs=[
            pl.BlockSpec((None, T, w), lambda b, p, i: (b, i, CB_SBQ // pp + p)),
            pl.BlockSpec((None, S, w), lambda b, p, i: (b, 0, CB_SBK // pp + p)),
            pl.BlockSpec((None, S, w), lambda b, p, i: (b, 0, CB_SBV // pp + p)),
            pl.BlockSpec((None, T, w), lambda b, p, i: (b, i, CB_SBZ // pp + p)),
        ],
        out_specs=pl.BlockSpec((None, T, w), lambda b, p, i: (b, i, p)),
        out_shape=jax.ShapeDtypeStruct((B, S, pairs * LANES), BF16),
        compiler_params=pltpu.CompilerParams(
            dimension_semantics=("parallel", "parallel", "arbitrary"), vmem_limit_bytes=VMEM_LIMIT),
        name="stick_breaking",
    )(proj_a, proj_a, proj_a, proj_a)


def _prep_kernel(q_ref, ks_ref, vs_ref, kw_ref, vw_ref, cos_ref, sin_ref, qg_ref, ksg_ref, kwg_ref,
                 qo_ref, kso_ref, vso_ref, kwo_ref, vwo_ref):
    tm = q_ref.shape[0]
    cos = cos_ref[...]
    sin = sin_ref[...]
    for p in range(NSA_HEADS // 2):
        x = q_ref[:, p * LANES:(p + 1) * LANES]
        qo_ref[:, p * LANES:(p + 1) * LANES] = (
            _norm_rope(x, qg_ref[...], cos, sin) * Q_SCALE2).astype(BF16)
    pos = pl.program_id(1) * tm + _iota((tm, LANES), 0)
    own_block = jnp.where(_iota((tm, LANES), 1) == pos // SLC_BLOCK, -BIG, 0.0).astype(BF16)
    ones = jnp.ones((tm, LANES), BF16)
    for g, half in enumerate(_dup_halves(_norm_rope(ks_ref[...], ksg_ref[...], cos, sin))):
        kso_ref[g] = jnp.concatenate([half.astype(BF16), own_block], axis=1)
    for g, half in enumerate(_dup_halves(_norm_rope(kw_ref[...], kwg_ref[...], cos, sin))):
        kwo_ref[g] = half.astype(BF16)
    for src, dst in ((vs_ref, vso_ref), (vw_ref, vwo_ref)):
        for g, half in enumerate(_dup_halves(src[...].astype(F32))):
            dst[g] = jnp.concatenate([half.astype(BF16), ones], axis=1)


def _nsa_prep(proj_a, proj_b, cos, sin, q_g, ks_g, kw_g):
    B, S, _ = proj_b.shape
    tm = 256
    G = NSA_KV_HEADS
    col = lambda cb, w: pl.BlockSpec((None, tm, w * LANES), lambda b, i: (b, i, cb // w))
    tab = pl.BlockSpec((tm, LANES), lambda b, i: (i, 0))
    gain = pl.BlockSpec((1, LANES), lambda b, i: (0, 0))
    kv_out = lambda w: pl.BlockSpec((None, G, tm, w * LANES), lambda b, i: (b, 0, i, 0))
    kv_shape = lambda w: jax.ShapeDtypeStruct((B, G, S, w * LANES), BF16)
    return pl.pallas_call(
        _prep_kernel,
        grid=(B, S // tm),
        in_specs=[col(CB_NQ, 4), col(CB_KS, 1), col(CB_VS, 1), col(CB_KW, 1), col(CB_VW, 1),
                  tab, tab, gain, gain, gain],
        out_specs=[pl.BlockSpec((None, tm, 4 * LANES), lambda b, i: (b, i, 0)),
                   kv_out(2), kv_out(2), kv_out(1), kv_out(2)],
        out_shape=[jax.ShapeDtypeStruct((B, S, 4 * LANES), BF16),
                   kv_shape(2), kv_shape(2), kv_shape(1), kv_shape(2)],
        compiler_params=pltpu.CompilerParams(
            dimension_semantics=("parallel", "parallel"), vmem_limit_bytes=VMEM_LIMIT),
        name="nsa_prep",
    )(proj_b, proj_b, proj_a, proj_b, proj_a, cos, sin, q_g, ks_g, kw_g)


def _compress_kernel(xk_ref, xv_ref, pek_ref, pev_ref, w1k_ref, w1v_ref, w2k_ref, w2v_ref,
                     cos_ref, sin_ref, g_ref, ko_ref, vo_ref):
    nrow = xk_ref.shape[0]
    half = CMP_STRIDE * LANES

    def mlp(x_ref, pe_ref, w1_ref, w2_ref):
        x = x_ref[...]
        first = _dot((x + pe_ref[0:1, :]).astype(BF16), w1_ref[0:half, :])
        second = _dot((x + pe_ref[1:2, :]).astype(BF16), w1_ref[half:2 * half, :])
        pre = first + pltpu.roll(second, nrow - 1, 0)
        return _dot(_silu(pre).astype(BF16), w2_ref[...])

    kc = _norm_rope(mlp(xk_ref, pek_ref, w1k_ref, w2k_ref), g_ref[...], cos_ref[...], sin_ref[...])
    a, b = _dup_halves(kc)
    ko_ref[0] = a.astype(BF16)
    ko_ref[1] = b.astype(BF16)
    a, b = _dup_halves(mlp(xv_ref, pev_ref, w1v_ref, w2v_ref))
    vo_ref[0] = a.astype(BF16)
    vo_ref[1] = b.astype(BF16)


def _expand_cmp_weights(pe, w1, w2):
    G = NSA_KV_HEADS
    eye = jnp.eye(G, dtype=w1.dtype)
    w1h = w1.reshape(2, CMP_STRIDE, HEAD_DIM, CMP_HIDDEN)
    w1e = jnp.einsum('aldh,gk->algdkh', w1h, eye).reshape(2 * CMP_STRIDE * G * HEAD_DIM, G * CMP_HIDDEN)
    w2e = jnp.einsum('hd,gk->ghkd', w2, eye).reshape(G * CMP_HIDDEN, G * HEAD_DIM)
    pee = jnp.broadcast_to(pe.reshape(2, CMP_STRIDE, 1, HEAD_DIM), (2, CMP_STRIDE, G, HEAD_DIM))
    return pee.reshape(2, CMP_STRIDE * G * HEAD_DIM), w1e.astype(BF16), w2e.astype(BF16)


def _compress(xk, xv, wk, wv, cos_c, sin_c, kc_g):
    B, nrow, width = xk.shape
    G = NSA_KV_HEADS
    full = lambda a: pl.BlockSpec(a.shape, lambda b: (0,) * a.ndim)
    xspec = pl.BlockSpec((None, nrow, width), lambda b: (b, 0, 0))
    ospec = pl.BlockSpec((None, G, nrow, LANES), lambda b: (b, 0, 0, 0))
    oshape = jax.ShapeDtypeStruct((B, G, nrow, LANES), BF16)
    args = (xk, xv, wk[0], wv[0], wk[1], wv[1], wk[2], wv[2], cos_c, sin_c, kc_g)
    return pl.pallas_call(
        _compress_kernel,
        grid=(B,),
        in_specs=[xspec, xspec] + [full(a) for a in args[2:]],
        out_specs=[ospec, ospec],
        out_shape=[oshape, oshape],
        compiler_params=pltpu.CompilerParams(
            dimension_semantics=("parallel",), vmem_limit_bytes=VMEM_LIMIT),
        name="nsa_compress",
    )(*args)


def _nsa_kernel(q_ref, kc_ref, vc_ref, ks_ref, vs_ref, kw_ref, vw_ref, ng_ref, nz_ref, o_ref, s_ref, *, seq):
    i = pl.program_id(1)
    G = NSA_KV_HEADS
    R = NSA_GROUP * TQ
    n_pairs = NSA_GROUP // 2
    n_blocks = seq // SLC_BLOCK
    n_cmp = (seq - CMP_LEN) // CMP_STRIDE + 1
    nc = kc_ref.shape[1]
    t0 = i * TQ

    lane_q = _iota((TQ, LANES), 1)
    low = lane_q < HEAD_DIM
    zero = jnp.zeros((TQ, LANES), BF16)
    onehot_t = (_iota((R, LANES), 1) == _iota((R, LANES), 0) % TQ).astype(BF16)

    def masked_bias(allowed):
        return jnp.where(allowed, 0.0, -BIG).astype(BF16)

    n_i = _iota((nc, LANES), 0)
    t_i = t0 + _iota((nc, LANES), 1)
    bias_c = masked_bias((CMP_STRIDE * n_i + (CMP_LEN - 1) <= t_i) & (n_i < n_cmp))
    has_valid = t0 + _iota((R, nc), 0) % TQ >= CMP_LEN - 1

    span = WINDOW + TQ
    start = pl.multiple_of(jnp.clip(t0 - WINDOW, 0, seq - span), TQ)
    k_pos = start + _iota((span, LANES), 0)
    t_w = t0 + _iota((span, LANES), 1)
    bias_w = masked_bias((k_pos <= t_w) & (k_pos > t_w - WINDOW))

    jn = _iota((LANES, nc), 0)
    nn = _iota((LANES, nc), 1)
    overlap_t = ((CMP_STRIDE * nn < SLC_BLOCK * (jn + 1)) & (CMP_STRIDE * nn + CMP_LEN > SLC_BLOCK * jn)
                 & (nn < n_cmp) & (jn < n_blocks)).astype(BF16)
    nb8 = -(-n_blocks // 8) * 8
    jb = _iota((nb8, TQ), 0)
    cur = (t0 + _iota((nb8, TQ), 1)) // SLC_BLOCK
    eye = (_iota((TQ, TQ), 0) == _iota((TQ, TQ), 1)).astype(BF16)

    o_cmp_g, o_win_g, qs_sel_g = [], [], []
    for g in range(G):
        slabs = []
        for p in range(n_pairs):
            qp = q_ref[:, (g * n_pairs + p) * LANES:(g * n_pairs + p + 1) * LANES]
            slabs += [jnp.where(low, qp, zero), jnp.where(low, zero, qp)]
        qs = jnp.concatenate(slabs, axis=0)
        qs_pos = jnp.concatenate([qs, onehot_t], axis=1)

        s_c = _dot_nt(qs_pos, jnp.concatenate([kc_ref[g], bias_c], axis=1))
        e = jnp.exp2(s_c - jnp.max(s_c, axis=-1, keepdims=True))
        p_c = jnp.where(has_valid, e / jnp.sum(e, axis=-1, keepdims=True), 0.0)
        o_cmp_g.append(_dot(p_c.astype(BF16), vc_ref[g]))

        s_w = _dot_nt(qs_pos, jnp.concatenate([kw_ref[g, pl.ds(start, span), :], bias_w], axis=1))
        ew = jnp.exp2(s_w - jnp.max(s_w, axis=-1, keepdims=True)).astype(BF16)
        ow = _dot(ew, vw_ref[g, pl.ds(start, span), :])
        o_win_g.append(ow[:, :LANES] / ow[:, LANES:])

        p_sum = p_c[0:TQ]
        for h in range(1, NSA_GROUP):
            p_sum = p_sum + p_c[h * TQ:(h + 1) * TQ]
        p_hi, p_lo = _split(p_sum)
        imp = (_dot_nt(overlap_t, p_hi) + _dot_nt(overlap_t, p_lo))[:nb8]
        imp = jnp.where(jb > cur, NEG, imp)
        imp = jnp.where((jb == 0) | (jb == cur - 1), FORCE, imp)
        imp = jnp.where(jb == cur, 2.0 * FORCE, imp)
        rank = jnp.zeros((nb8, TQ), jnp.int32)
        for j in range(n_blocks):
            row = jnp.broadcast_to(imp[j:j + 1, :], (nb8, TQ))
            ahead = (row > imp) | ((row == imp) & (jb > j))
            rank = rank + ahead.astype(jnp.int32)
        chosen_t = (rank < min(SLC_TOPK, n_blocks)) & (jb <= cur) & (jb < n_blocks)
        not_t = jnp.where(chosen_t, 0.0, 1.0)
        if nb8 < LANES:
            not_t = jnp.concatenate([not_t, jnp.zeros((LANES - nb8, TQ), F32)], axis=0)
        not_chosen = _dot_nt(eye, not_t.astype(BF16)).astype(BF16)
        qs_sel_g.append(jnp.concatenate([qs, jnp.concatenate([not_chosen] * NSA_GROUP, axis=0)], axis=1))

    last = (t0 + TQ - 1) // SLC_TK

    def chunk(ref, g, c):
        return ref[g, pl.ds(pl.multiple_of(c * SLC_TK, SLC_TK), SLC_TK), :]

    def fold(s):
        out = s[:, :LANES]
        for k in range(1, SLC_TK // LANES):
            out = jnp.maximum(out, s[:, k * LANES:(k + 1) * LANES])
        return out

    def pass1(c, run):
        out = []
        for g in range(G):
            s = _dot_nt(qs_sel_g[g], chunk(ks_ref, g, c))
            s_ref[g, c] = s
            out.append(jnp.maximum(run[g], fold(s)))
        return tuple(out)

    run = lax.fori_loop(0, last, pass1, (jnp.full((R, LANES), -3.0e38, F32),) * G)
    causal = last * SLC_TK + _iota((R, SLC_TK), 1) <= t0 + _iota((R, SLC_TK), 0) % TQ
    m_sel = []
    for g in range(G):
        s = jnp.where(causal, _dot_nt(qs_sel_g[g], chunk(ks_ref, g, last)), -BIG)
        s_ref[g, last] = s
        m_sel.append(jnp.broadcast_to(jnp.max(jnp.maximum(run[g], fold(s)), axis=-1, keepdims=True), (R, SLC_TK)))

    def pass2(c, acc):
        out = []
        for g in range(G):
            p = jnp.exp2(s_ref[g, c] - m_sel[g]).astype(BF16)
            out.append(acc[g] + _dot(p, chunk(vs_ref, g, c)))
        return tuple(out)

    acc = lax.fori_loop(0, last + 1, pass2, (jnp.zeros((R, 2 * LANES), F32),) * G)

    sig_hi, sig_lo = _split(_sigmoid(ng_ref[...]))
    width = 3 * n_pairs * LANES
    col = _iota((LANES, width), 1)
    for g in range(G):
        o_slc = acc[g][:, :LANES] / acc[g][:, LANES:]
        want = (col // (n_pairs * LANES)) * NSA_HEADS + g * NSA_GROUP + 2 * ((col // LANES) % n_pairs) \
            + (col % LANES) // HEAD_DIM
        pick = (_iota((LANES, width), 0) == want).astype(BF16)
        gates = _dot(sig_hi, pick) + _dot(sig_lo, pick)
        for p in range(n_pairs):
            total = None
            for br, o in enumerate((o_cmp_g[g], o_slc, o_win_g[g])):
                pair = jnp.where(low, o[2 * p * TQ:(2 * p + 1) * TQ], o[(2 * p + 1) * TQ:(2 * p + 2) * TQ])
                c0 = (br * n_pairs + p) * LANES
                term = gates[:, c0:c0 + LANES] * pair
                total = term if total is None else total + term
            cols = slice((g * n_pairs + p) * LANES, (g * n_pairs + p + 1) * LANES)
            o_ref[:, cols] = (total * _silu(nz_ref[:, cols].astype(F32))).astype(BF16)


def _nsa_core(qn, kc, vc, ks, vs, kw, vw, proj_a, proj_b):
    B, S, _ = proj_b.shape
    G = NSA_KV_HEADS
    width = NSA_HEADS * HEAD_DIM
    ncmp = kc.shape[2]
    kvc = pl.BlockSpec((None, G, ncmp, LANES), lambda b, i: (b, 0, 0, 0))
    kvs = lambda w: pl.BlockSpec((None, G, S, w * LANES), lambda b, i: (b, 0, 0, 0))
    return pl.pallas_call(
        functools.partial(_nsa_kernel, seq=S),
        grid=(B, S // TQ),
        in_specs=[
            pl.BlockSpec((None, TQ, width), lambda b, i: (b, i, 0)),
            kvc, kvc, kvs(2), kvs(2), kvs(1), kvs(2),
            pl.BlockSpec((None, TQ, LANES), lambda b, i: (b, i, CB_NG)),
            pl.BlockSpec((None, TQ, width), lambda b, i: (b, i, CB_NZ * LANES // width)),
        ],
        out_specs=pl.BlockSpec((None, TQ, width), lambda b, i: (b, i, 0)),
        out_shape=jax.ShapeDtypeStruct((B, S, width), BF16),
        scratch_shapes=[pltpu.VMEM((G, S // SLC_TK, NSA_GROUP * TQ, SLC_TK), F32)],
        compiler_params=pltpu.CompilerParams(
            dimension_semantics=("parallel", "arbitrary"), vmem_limit_bytes=VMEM_LIMIT),
        name="nsa_core",
    )(qn, kc, vc, ks, vs, kw, vw, proj_b, proj_a)


def _merge_kernel(a_ref, b_ref, ma_ref, mb_ref, x_ref, gate_ref, wa_ref, wb_ref, wo_ref, o_ref):
    ya = _dot(a_ref[...], wa_ref[...])
    yb = _dot(b_ref[...], wb_ref[...])
    y = _sigmoid(ma_ref[...].astype(F32)) * ya + _sigmoid(mb_ref[...].astype(F32)) * yb
    o_ref[...] = x_ref[...] + gate_ref[...] * _dot(y.astype(BF16), wo_ref[...])


def _merge(a2, b2, proj_a2, x2, mod_l, wa, wb, wo, seq):
    M, D = x2.shape
    tm = 256
    per_b = seq // tm
    B = mod_l.shape[1]
    mod4 = mod_l.reshape(3, B, 1, D)
    dcols = D // LANES
    full = lambda a: pl.BlockSpec(a.shape, lambda i: (0, 0))
    return pl.pallas_call(
        _merge_kernel,
        grid=(M // tm,),
        in_specs=[
            pl.BlockSpec((tm, a2.shape[1]), lambda i: (i, 0)),
            pl.BlockSpec((tm, b2.shape[1]), lambda i: (i, 0)),
            pl.BlockSpec((tm, D), lambda i: (i, CB_MA // dcols)),
            pl.BlockSpec((tm, D), lambda i: (i, CB_MB // dcols)),
            pl.BlockSpec((tm, D), lambda i: (i, 0)),
            pl.BlockSpec((None, None, 1, D), lambda i: (2, i // per_b, 0, 0)),
            full(wa), full(wb), full(wo),
        ],
        out_specs=pl.BlockSpec((tm, D), lambda i: (i, 0)),
        out_shape=jax.ShapeDtypeStruct((M, D), F32),
        compiler_params=pltpu.CompilerParams(
            dimension_semantics=("parallel",), vmem_limit_bytes=VMEM_LIMIT),
        name="merge_out",
    )(a2, b2, proj_a2, proj_a2, x2, mod4, wa, wb, wo)


def _permute_w_in(w_in):
    sb, nw, kv = SB_HEADS * HEAD_DIM, NSA_HEADS * HEAD_DIM, NSA_KV_HEADS * HEAD_DIM
    D = w_in.shape[1]
    sizes = (sb, sb, sb, sb, nw, kv, kv, kv, kv, kv, kv, nw, 3 * NSA_HEADS, D, D)
    offs = np.concatenate([[0], np.cumsum(sizes)])
    assert offs[-1] == w_in.shape[2]
    w_b = w_in.astype(BF16)
    seg = lambda k: w_b[:, :, offs[k]:offs[k + 1]]
    pad = lambda n: jnp.zeros(w_in.shape[:2] + (n,), BF16)
    sb_q = (w_in[:, :, offs[0]:offs[1]] * Q_SCALE2).astype(BF16)
    part_a = jnp.concatenate([sb_q, seg(1), seg(2), seg(3), seg(13), seg(14), seg(11),
                              seg(6), seg(8), seg(10), pad(LANES)], axis=-1)
    part_b = jnp.concatenate([seg(4), seg(5), seg(7), seg(9), seg(12), pad(LANES - 3 * NSA_HEADS)], axis=-1)
    assert part_a.shape[-1] == N_PROJ_A and part_b.shape[-1] == N_PROJ_B
    return jnp.concatenate([part_a, part_b], axis=-1)


def _rope_tables(pos):
    half = HEAD_DIM // 2
    freq = ROPE_THETA ** (-jnp.arange(half, dtype=F32) / half)
    ang = pos.astype(F32)[:, None] * freq[None, :]
    cos, sin = jnp.cos(ang), jnp.sin(ang)
    reps = LANES // HEAD_DIM
    return jnp.tile(jnp.concatenate([cos, cos], -1), (1, reps)), jnp.tile(jnp.concatenate([-sin, sin], -1), (1, reps))


def kernel(x, c, ada_w, ada_b, norm_g, w_in, q_norm_g, kc_norm_g, ks_norm_g, kw_norm_g, cmp_pe_k, cmp_w1_k, cmp_w2_k, cmp_pe_v, cmp_w1_v, cmp_w2_v, w_proj_a, w_proj_b, w_out):
    B, S, D = x.shape
    L = ada_w.shape[0]
    M = B * S
    assert D == 8 * LANES and S % 1024 == 0 and S // SLC_BLOCK >= SLC_TOPK and S >= WINDOW + TQ
    assert TQ == LANES and S // SLC_BLOCK <= LANES and S % SLC_TK == 0 and S % SB_T == 0
    n_cmp_rows = S // CMP_STRIDE

    mod = _modulation(c, ada_w, ada_b)
    w_p = _permute_w_in(w_in)
    wa, wb, wo = w_proj_a.astype(BF16), w_proj_b.astype(BF16), w_out.astype(BF16)
    cos, sin = _rope_tables(jnp.arange(S))
    cos_c, sin_c = _rope_tables(CMP_STRIDE * jnp.arange(n_cmp_rows) + CMP_LEN - 1)
    tile2 = lambda g: jnp.tile(g, (1, LANES // HEAD_DIM)).reshape(L, 1, LANES)
    q_g, kc_g, ks_g, kw_g = tile2(q_norm_g), tile2(kc_norm_g), tile2(ks_norm_g), tile2(kw_norm_g)

    x2 = x.reshape(M, D)
    for l in range(L):
        proj_a2, proj_b2 = _in_projection(x2, mod[l], norm_g[l], w_p[l], S)
        proj_a = proj_a2.reshape(B, S, N_PROJ_A)
        proj_b = proj_b2.reshape(B, S, N_PROJ_B)
        a = _sb_attention(proj_a)
        qn, ks, vs, kw, vw = _nsa_prep(proj_a, proj_b, cos, sin, q_g[l], ks_g[l], kw_g[l])
        xk = proj_b[:, :, CB_KC * LANES:(CB_KC + 1) * LANES].reshape(B, n_cmp_rows, CMP_STRIDE * LANES)
        xv = proj_a[:, :, CB_VC * LANES:(CB_VC + 1) * LANES].reshape(B, n_cmp_rows, CMP_STRIDE * LANES)
        kc, vc = _compress(xk, xv,
                           _expand_cmp_weights(cmp_pe_k[l], cmp_w1_k[l], cmp_w2_k[l]),
                           _expand_cmp_weights(cmp_pe_v[l], cmp_w1_v[l], cmp_w2_v[l]),
                           cos_c, sin_c, kc_g[l])
        b = _nsa_core(qn, kc, vc, ks, vs, kw, vw, proj_a, proj_b)
        x2 = _merge(a.reshape(M, -1), b.reshape(M, -1), proj_a2, x2, mod[l], wa[l], wb[l], wo[l], S)
    return x2.reshape(B, S, D)
```

```python
import functools
import math

import jax
import jax.numpy as jnp
import numpy as np
from jax import lax
from jax.experimental import pallas as pl
from jax.experimental.pallas import tpu as pltpu

HEAD_DIM = 64
LANES = 128
SB_HEADS = 8
NSA_HEADS = 8
NSA_KV_HEADS = 2
NSA_GROUP = NSA_HEADS // NSA_KV_HEADS
CMP_LEN = 32
CMP_STRIDE = 16
CMP_HIDDEN = 128
SLC_BLOCK = 64
SLC_TOPK = 8
WINDOW = 256
ROPE_THETA = 10000.0
EPS = 1e-6
NEG = -1e30
FORCE = 1e4
Q_SCALE2 = HEAD_DIM ** -0.5 * math.log2(math.e)
BIG = 2.0 ** 100

F32 = jnp.float32
BF16 = jnp.bfloat16

CB_SBQ, CB_SBK, CB_SBV, CB_SBZ = 0, 4, 8, 12
CB_MA, CB_MB, CB_NZ, CB_VC, CB_VS, CB_VW = 16, 24, 32, 36, 37, 38
N_PROJ_A = 40 * LANES
CB_NQ, CB_KC, CB_KS, CB_KW, CB_NG = 0, 4, 5, 6, 7
N_PROJ_B = 8 * LANES

TQ = 128
SLC_TK = 1024
SB_T = 512
SB_PAIRS = 4
VMEM_LIMIT = 56 * 1024 * 1024


def _dot(a, b):
    return jnp.dot(a, b, preferred_element_type=F32)


def _dot_nt(a, b):
    return lax.dot_general(a, b, (((1,), (1,)), ((), ())), preferred_element_type=F32)


def _split(x):
    hi = x.astype(BF16)
    lo = (x - hi.astype(F32)).astype(BF16)
    return hi, lo


def _dot_hl(x, m):
    hi, lo = _split(x)
    return _dot(hi, m) + _dot(lo, m)


def _dot3(a, b):
    ah, al = _split(a)
    bh, bl = _split(b)
    return _dot(ah, bh) + _dot(al, bh) + _dot(ah, bl)


def _sigmoid(x):
    return 1.0 / (1.0 + jnp.exp(-x))


def _silu(x):
    return x * _sigmoid(x)


def _iota(shape, dim):
    return lax.broadcasted_iota(jnp.int32, shape, dim)


def _head_block_ones():
    r = _iota((LANES, LANES), 0) // HEAD_DIM
    c = _iota((LANES, LANES), 1) // HEAD_DIM
    return (r == c).astype(BF16)


def _norm_rope(x, g, cos, sin_signed):
    ss = _dot_hl(x * x, _head_block_ones())
    y = x * lax.rsqrt(ss * (1.0 / HEAD_DIM) + EPS) * g
    lane = _iota(y.shape, 1)
    first_half = (lane % HEAD_DIM) < (HEAD_DIM // 2)
    swapped = jnp.where(first_half, pltpu.roll(y, LANES - HEAD_DIM // 2, 1),
                        pltpu.roll(y, HEAD_DIM // 2, 1))
    return y * cos + swapped * sin_signed


def _dup_halves(x):
    lane = _iota(x.shape, 1)
    r = pltpu.roll(x, HEAD_DIM, 1)
    low = lane < HEAD_DIM
    return jnp.where(low, x, r), jnp.where(low, r, x)


def _mod_kernel(c_ref, w_ref, b_ref, o_ref):
    o_ref[...] = _dot3(_silu(c_ref[...]), w_ref[...]) + b_ref[...]


def _modulation(c, ada_w, ada_b):
    L, D, _ = ada_w.shape
    B = c.shape[0]
    return pl.pallas_call(
        _mod_kernel,
        grid=(L, 3),
        in_specs=[
            pl.BlockSpec((B, D), lambda l, j: (0, 0)),
            pl.BlockSpec((None, D, D), lambda l, j: (l, 0, j)),
            pl.BlockSpec((None, 1, D), lambda l, j: (l, 0, j)),
        ],
        out_specs=pl.BlockSpec((None, None, B, D), lambda l, j: (l, j, 0, 0)),
        out_shape=jax.ShapeDtypeStruct((L, 3, B, D), F32),
        compiler_params=pltpu.CompilerParams(dimension_semantics=("arbitrary", "arbitrary")),
        name="adaln_mod",
    )(c, ada_w, ada_b.reshape(L, 1, 3 * D))


def _inproj_kernel(x_ref, shift_ref, scale_ref, g_ref, w_ref, oa_ref, ob_ref, h_ref):
    j = pl.program_id(1)
    n_a = pl.num_programs(1) - 1

    @pl.when(j == 0)
    def _():
        x = x_ref[...]
        ms = jnp.mean(x * x, axis=-1, keepdims=True)
        y = x * lax.rsqrt(ms + EPS) * g_ref[...]
        h_ref[...] = (y * (1.0 + scale_ref[...]) + shift_ref[...]).astype(BF16)

    @pl.when(j < n_a)
    def _():
        oa_ref[...] = _dot(h_ref[...], w_ref[...]).astype(BF16)

    @pl.when(j == n_a)
    def _():
        ob_ref[...] = _dot(h_ref[...], w_ref[...])


def _in_projection(x2, mod_l, norm_g, w_l, seq):
    M, D = x2.shape
    tm, tn = 1024, N_PROJ_B
    n_a = N_PROJ_A // tn
    per_b = seq // tm
    B = mod_l.shape[1]
    mod4 = mod_l.reshape(3, B, 1, D)
    return pl.pallas_call(
        _inproj_kernel,
        grid=(M // tm, n_a + 1),
        in_specs=[
            pl.BlockSpec((tm, D), lambda i, j: (i, 0)),
            pl.BlockSpec((None, None, 1, D), lambda i, j: (0, i // per_b, 0, 0)),
            pl.BlockSpec((None, None, 1, D), lambda i, j: (1, i // per_b, 0, 0)),
            pl.BlockSpec((1, D), lambda i, j: (0, 0)),
            pl.BlockSpec((D, tn), lambda i, j: (0, j)),
        ],
        out_specs=[pl.BlockSpec((tm, tn), lambda i, j: (i, jnp.minimum(j, n_a - 1))),
                   pl.BlockSpec((tm, tn), lambda i, j: (i, 0))],
        out_shape=[jax.ShapeDtypeStruct((M, N_PROJ_A), BF16), jax.ShapeDtypeStruct((M, N_PROJ_B), F32)],
        scratch_shapes=[pltpu.VMEM((tm, D), BF16)],
        compiler_params=pltpu.CompilerParams(
            dimension_semantics=("parallel", "arbitrary"), vmem_limit_bytes=VMEM_LIMIT),
        name="in_projection",
    )(x2, mod4, mod4, norm_g.reshape(1, D), w_l)


def _sb_kernel(q_ref, k_ref, v_ref, z_ref, o_ref):
    T = SB_T
    i = pl.program_id(2)
    n_pair = q_ref.shape[1] // LANES
    lane = _iota((T, LANES), 1)
    low = lane < HEAD_DIM
    q_heads = []
    for p in range(n_pair):
        q2 = q_ref[:, p * LANES:(p + 1) * LANES]
        zero = jnp.zeros_like(q2)
        q_heads += [jnp.where(low, q2, zero), jnp.where(low, zero, q2)]

    suffix = (_iota((T, T), 0) > _iota((T, T), 1)).astype(BF16)
    strict = _iota((T, T), 1) < _iota((T, T), 0)

    def chain(qh, k, v, acc, carry, diagonal):
        z = _dot_nt(qh, k)
        nz = -z
        log_keep = jnp.minimum(nz, 0.0) - jnp.log2(1.0 + jnp.exp2(jnp.minimum(z, nz)))
        log_hit = z + log_keep
        if diagonal:
            log_keep = jnp.where(strict, log_keep, 0.0)
        keep_b = log_keep.astype(BF16)
        within = _dot(keep_b, suffix)
        w = jnp.exp2(log_hit + within + carry)
        if diagonal:
            w = jnp.where(strict, w, 0.0)
        row_total = within[:, 0:1] + keep_b[:, 0:1].astype(F32)
        return acc + _dot(w.astype(BF16), v), carry + row_total

    def block(kb, state, diagonal):
        rows = pl.ds(pl.multiple_of(kb * T, T), T)
        out = []
        for h in range(2 * n_pair):
            cols = slice((h // 2) * LANES, (h // 2 + 1) * LANES)
            out += chain(q_heads[h], k_ref[rows, cols], v_ref[rows, cols], state[2 * h], state[2 * h + 1], diagonal)
        return tuple(out)

    acc0 = jnp.zeros((T, LANES), F32)
    carry0 = jnp.zeros((T, 1), F32)
    state = block(i, (acc0, carry0) * (2 * n_pair), True)
    state = lax.fori_loop(0, i, lambda jj, st: block(i - 1 - jj, st, False), state)
    for p in range(n_pair):
        out = jnp.where(low, state[4 * p], state[4 * p + 2])
        cols = slice(p * LANES, (p + 1) * LANES)
        o_ref[:, cols] = (out * _silu(z_ref[:, cols].astype(F32))).astype(BF16)


def _sb_attention(proj_a):
    B, S, _ = proj_a.shape
    pairs = SB_HEADS // 2
    T = SB_T
    pp = SB_PAIRS
    w = pp * LANES
    return pl.pallas_call(
        _sb_kernel,
        grid=(B, pairs // pp, S // T),
        in_specs=[
            pl.BlockSpec((None, T, w), lambda b, p, i: (b, i, CB_SBQ // pp + p)),
            pl.BlockSpec((None, S, w), lambda b, p, i: (b, 0, CB_SBK // pp + p)),
            pl.BlockSpec((None, S, w), lambda b, p, i: (b, 0, CB_SBV // pp + p)),
            pl.BlockSpec((None, T, w), lambda b, p, i: (b, i, CB_SBZ // pp + p)),
        ],
        out_specs=pl.BlockSpec((None, T, w), lambda b, p, i: (b, i, p)),
        out_shape=jax.ShapeDtypeStruct((B, S, pairs * LANES), BF16),
        compiler_params=pltpu.CompilerParams(
            dimension_semantics=("parallel", "parallel", "arbitrary"), vmem_limit_bytes=VMEM_LIMIT),
        name="stick_breaking",
    )(proj_a, proj_a, proj_a, proj_a)


def _prep_kernel(q_ref, ks_ref, vs_ref, kw_ref, vw_ref, cos_ref, sin_ref, qg_ref, ksg_ref, kwg_ref,
                 qo_ref, kso_ref, vso_ref, kwo_ref, vwo_ref):
    tm = q_ref.shape[0]
    cos = cos_ref[...]
    sin = sin_ref[...]
    for p in range(NSA_HEADS // 2):
        x = q_ref[:, p * LANES:(p + 1) * LANES]
        qo_ref[:, p * LANES:(p + 1) * LANES] = (
            _norm_rope(x, qg_ref[...], cos, sin) * Q_SCALE2).astype(BF16)
    pos = pl.program_id(1) * tm + _iota((tm, LANES), 0)
    own_block = jnp.where(_iota((tm, LANES), 1) == pos // SLC_BLOCK, -BIG, 0.0).astype(BF16)
    ones = jnp.ones((tm, LANES), BF16)
    for g, half in enumerate(_dup_halves(_norm_rope(ks_ref[...], ksg_ref[...], cos, sin))):
        kso_ref[g] = jnp.concatenate([half.astype(BF16), own_block], axis=1)
    for g, half in enumerate(_dup_halves(_norm_rope(kw_ref[...], kwg_ref[...], cos, sin))):
        kwo_ref[g] = half.astype(BF16)
    for src, dst in ((vs_ref, vso_ref), (vw_ref, vwo_ref)):
        for g, half in enumerate(_dup_halves(src[...].astype(F32))):
            dst[g] = jnp.concatenate([half.astype(BF16), ones], axis=1)


def _nsa_prep(proj_a, proj_b, cos, sin, q_g, ks_g, kw_g):
    B, S, _ = proj_b.shape
    tm = 256
    G = NSA_KV_HEADS
    col = lambda cb, w: pl.BlockSpec((None, tm, w * LANES), lambda b, i: (b, i, cb // w))
    tab = pl.BlockSpec((tm, LANES), lambda b, i: (i, 0))
    gain = pl.BlockSpec((1, LANES), lambda b, i: (0, 0))
    kv_out = lambda w: pl.BlockSpec((None, G, tm, w * LANES), lambda b, i: (b, 0, i, 0))
    kv_shape = lambda w: jax.ShapeDtypeStruct((B, G, S, w * LANES), BF16)
    return pl.pallas_call(
        _prep_kernel,
        grid=(B, S // tm),
        in_specs=[col(CB_NQ, 4), col(CB_KS, 1), col(CB_VS, 1), col(CB_KW, 1), col(CB_VW, 1),
                  tab, tab, gain, gain, gain],
        out_specs=[pl.BlockSpec((None, tm, 4 * LANES), lambda b, i: (b, i, 0)),
                   kv_out(2), kv_out(2), kv_out(1), kv_out(2)],
        out_shape=[jax.ShapeDtypeStruct((B, S, 4 * LANES), BF16),
                   kv_shape(2), kv_shape(2), kv_shape(1), kv_shape(2)],
        compiler_params=pltpu.CompilerParams(
            dimension_semantics=("parallel", "parallel"), vmem_limit_bytes=VMEM_LIMIT),
        name="nsa_prep",
    )(proj_b, proj_b, proj_a, proj_b, proj_a, cos, sin, q_g, ks_g, kw_g)


def _compress_kernel(xk_ref, xv_ref, pek_ref, pev_ref, w1k_ref, w1v_ref, w2k_ref, w2v_ref,
                     cos_ref, sin_ref, g_ref, ko_ref, vo_ref):
    nrow = xk_ref.shape[0]
    half = CMP_STRIDE * LANES

    def mlp(x_ref, pe_ref, w1_ref, w2_ref):
        x = x_ref[...]
        first = _dot((x + pe_ref[0:1, :]).astype(BF16), w1_ref[0:half, :])
        second = _dot((x + pe_ref[1:2, :]).astype(BF16), w1_ref[half:2 * half, :])
        pre = first + pltpu.roll(second, nrow - 1, 0)
        return _dot(_silu(pre).astype(BF16), w2_ref[...])

    kc = _norm_rope(mlp(xk_ref, pek_ref, w1k_ref, w2k_ref), g_ref[...], cos_ref[...], sin_ref[...])
    a, b = _dup_halves(kc)
    ko_ref[0] = a.astype(BF16)
    ko_ref[1] = b.astype(BF16)
    a, b = _dup_halves(mlp(xv_ref, pev_ref, w1v_ref, w2v_ref))
    vo_ref[0] = a.astype(BF16)
    vo_ref[1] = b.astype(BF16)


def _expand_cmp_weights(pe, w1, w2):
    G = NSA_KV_HEADS
    eye = jnp.eye(G, dtype=w1.dtype)
    w1h = w1.reshape(2, CMP_STRIDE, HEAD_DIM, CMP_HIDDEN)
    w1e = jnp.einsum('aldh,gk->algdkh', w1h, eye).reshape(2 * CMP_STRIDE * G * HEAD_DIM, G * CMP_HIDDEN)
    w2e = jnp.einsum('hd,gk->ghkd', w2, eye).reshape(G * CMP_HIDDEN, G * HEAD_DIM)
    pee = jnp.broadcast_to(pe.reshape(2, CMP_STRIDE, 1, HEAD_DIM), (2, CMP_STRIDE, G, HEAD_DIM))
    return pee.reshape(2, CMP_STRIDE * G * HEAD_DIM), w1e.astype(BF16), w2e.astype(BF16)


def _compress(xk, xv, wk, wv, cos_c, sin_c, kc_g):
    B, nrow, width = xk.shape
    G = NSA_KV_HEADS
    full = lambda a: pl.BlockSpec(a.shape, lambda b: (0,) * a.ndim)
    xspec = pl.BlockSpec((None, nrow, width), lambda b: (b, 0, 0))
    ospec = pl.BlockSpec((None, G, nrow, LANES), lambda b: (b, 0, 0, 0))
    oshape = jax.ShapeDtypeStruct((B, G, nrow, LANES), BF16)
    args = (xk, xv, wk[0], wv[0], wk[1], wv[1], wk[2], wv[2], cos_c, sin_c, kc_g)
    return pl.pallas_call(
        _compress_kernel,
        grid=(B,),
        in_specs=[xspec, xspec] + [full(a) for a in args[2:]],
        out_specs=[ospec, ospec],
        out_shape=[oshape, oshape],
        compiler_params=pltpu.CompilerParams(
            dimension_semantics=("parallel",), vmem_limit_bytes=VMEM_LIMIT),
        name="nsa_compress",
    )(*args)


def _nsa_kernel(q_ref, kc_ref, vc_ref, ks_ref, vs_ref, kw_ref, vw_ref, ng_ref, nz_ref, o_ref, s_ref, *, seq):
    i = pl.program_id(1)
    G = NSA_KV_HEADS
    R = NSA_GROUP * TQ
    n_pairs = NSA_GROUP // 2
    n_blocks = seq // SLC_BLOCK
    n_cmp = (seq - CMP_LEN) // CMP_STRIDE + 1
    nc = kc_ref.shape[1]
    t0 = i * TQ

    lane_q = _iota((TQ, LANES), 1)
    low = lane_q < HEAD_DIM
    zero = jnp.zeros((TQ, LANES), BF16)
    onehot_t = (_iota((R, LANES), 1) == _iota((R, LANES), 0) % TQ).astype(BF16)

    def masked_bias(allowed):
        return jnp.where(allowed, 0.0, -BIG).astype(BF16)

    n_i = _iota((nc, LANES), 0)
    t_i = t0 + _iota((nc, LANES), 1)
    bias_c = masked_bias((CMP_STRIDE * n_i + (CMP_LEN - 1) <= t_i) & (n_i < n_cmp))
    has_valid = t0 + _iota((R, nc), 0) % TQ >= CMP_LEN - 1

    span = WINDOW + TQ
    start = pl.multiple_of(jnp.clip(t0 - WINDOW, 0, seq - span), TQ)
    k_pos = start + _iota((span, LANES), 0)
    t_w = t0 + _iota((span, LANES), 1)
    bias_w = masked_bias((k_pos <= t_w) & (k_pos > t_w - WINDOW))

    jn = _iota((LANES, nc), 0)
    nn = _iota((LANES, nc), 1)
    overlap_t = ((CMP_STRIDE * nn < SLC_BLOCK * (jn + 1)) & (CMP_STRIDE * nn + CMP_LEN > SLC_BLOCK * jn)
                 & (nn < n_cmp) & (jn < n_blocks)).astype(BF16)
    nb8 = -(-n_blocks // 8) * 8
    jb = _iota((nb8, TQ), 0)
    cur = (t0 + _iota((nb8, TQ), 1)) // SLC_BLOCK
    eye = (_iota((TQ, TQ), 0) == _iota((TQ, TQ), 1)).astype(BF16)

    o_cmp_g, o_win_g, qs_sel_g = [], [], []
    for g in range(G):
        slabs = []
        for p in range(n_pairs):
            qp = q_ref[:, (g * n_pairs + p) * LANES:(g * n_pairs + p + 1) * LANES]
            slabs += [jnp.where(low, qp, zero), jnp.where(low, zero, qp)]
        qs = jnp.concatenate(slabs, axis=0)
        qs_pos = jnp.concatenate([qs, onehot_t], axis=1)

        s_c = _dot_nt(qs_pos, jnp.concatenate([kc_ref[g], bias_c], axis=1))
        e = jnp.exp2(s_c - jnp.max(s_c, axis=-1, keepdims=True))
        p_c = jnp.where(has_valid, e / jnp.sum(e, axis=-1, keepdims=True), 0.0)
        o_cmp_g.append(_dot(p_c.astype(BF16), vc_ref[g]))

        s_w = _dot_nt(qs_pos, jnp.concatenate([kw_ref[g, pl.ds(start, span), :], bias_w], axis=1))
        ew = jnp.exp2(s_w - jnp.max(s_w, axis=-1, keepdims=True)).astype(BF16)
        ow = _dot(ew, vw_ref[g, pl.ds(start, span), :])
        o_win_g.append(ow[:, :LANES] / ow[:, LANES:])

        p_sum = p_c[0:TQ]
        for h in range(1, NSA_GROUP):
            p_sum = p_sum + p_c[h * TQ:(h + 1) * TQ]
        p_hi, p_lo = _split(p_sum)
        imp = (_dot_nt(overlap_t, p_hi) + _dot_nt(overlap_t, p_lo))[:nb8]
        imp = jnp.where(jb > cur, NEG, imp)
        imp = jnp.where((jb == 0) | (jb == cur - 1), FORCE, imp)
        imp = jnp.where(jb == cur, 2.0 * FORCE, imp)
        rank = jnp.zeros((nb8, TQ), jnp.int32)
        for j in range(n_blocks):
            row = jnp.broadcast_to(imp[j:j + 1, :], (nb8, TQ))
            ahead = (row > imp) | ((row == imp) & (jb > j))
            rank = rank + ahead.astype(jnp.int32)
        chosen_t = (rank < min(SLC_TOPK, n_blocks)) & (jb <= cur) & (jb < n_blocks)
        not_t = jnp.where(chosen_t, 0.0, 1.0)
        if nb8 < LANES:
            not_t = jnp.concatenate([not_t, jnp.zeros((LANES - nb8, TQ), F32)], axis=0)
        not_chosen = _dot_nt(eye, not_t.astype(BF16)).astype(BF16)
        qs_sel_g.append(jnp.concatenate([qs, jnp.concatenate([not_chosen] * NSA_GROUP, axis=0)], axis=1))

    last = (t0 + TQ - 1) // SLC_TK

    def chunk(ref, g, c):
        return ref[g, pl.ds(pl.multiple_of(c * SLC_TK, SLC_TK), SLC_TK), :]

    def fold(s):
        out = s[:, :LANES]
        for k in range(1, SLC_TK // LANES):
            out = jnp.maximum(out, s[:, k * LANES:(k + 1) * LANES])
        return out

    def pass1(c, run):
        out = []
        for g in range(G):
            s = _dot_nt(qs_sel_g[g], chunk(ks_ref, g, c))
            s_ref[g, c] = s
            out.append(jnp.maximum(run[g], fold(s)))
        return tuple(out)

    run = lax.fori_loop(0, last, pass1, (jnp.full((R, LANES), -3.0e38, F32),) * G)
    causal = last * SLC_TK + _iota((R, SLC_TK), 1) <= t0 + _iota((R, SLC_TK), 0) % TQ
    m_sel = []
    for g in range(G):
        s = jnp.where(causal, _dot_nt(qs_sel_g[g], chunk(ks_ref, g, last)), -BIG)
        s_ref[g, last] = s
        m_sel.append(jnp.broadcast_to(jnp.max(jnp.maximum(run[g], fold(s)), axis=-1, keepdims=True), (R, SLC_TK)))

    def pass2(c, acc):
        out = []
        for g in range(G):
            p = jnp.exp2(s_ref[g, c] - m_sel[g]).astype(BF16)
            out.append(acc[g] + _dot(p, chunk(vs_ref, g, c)))
        return tuple(out)

    acc = lax.fori_loop(0, last + 1, pass2, (jnp.zeros((R, 2 * LANES), F32),) * G)

    sig_hi, sig_lo = _split(_sigmoid(ng_ref[...]))
    width = 3 * n_pairs * LANES
    col = _iota((LANES, width), 1)
    for g in range(G):
        o_slc = acc[g][:, :LANES] / acc[g][:, LANES:]
        want = (col // (n_pairs * LANES)) * NSA_HEADS + g * NSA_GROUP + 2 * ((col // LANES) % n_pairs) \
            + (col % LANES) // HEAD_DIM
        pick = (_iota((LANES, width), 0) == want).astype(BF16)
        gates = _dot(sig_hi, pick) + _dot(sig_lo, pick)
        for p in range(n_pairs):
            total = None
            for br, o in enumerate((o_cmp_g[g], o_slc, o_win_g[g])):
                pair = jnp.where(low, o[2 * p * TQ:(2 * p + 1) * TQ], o[(2 * p + 1) * TQ:(2 * p + 2) * TQ])
                c0 = (br * n_pairs + p) * LANES
                term = gates[:, c0:c0 + LANES] * pair
                total = term if total is None else total + term
            cols = slice((g * n_pairs + p) * LANES, (g * n_pairs + p + 1) * LANES)
            o_ref[:, cols] = (total * _silu(nz_ref[:, cols].astype(F32))).astype(BF16)


def _nsa_core(qn, kc, vc, ks, vs, kw, vw, proj_a, proj_b):
    B, S, _ = proj_b.shape
    G = NSA_KV_HEADS
    width = NSA_HEADS * HEAD_DIM
    ncmp = kc.shape[2]
    kvc = pl.BlockSpec((None, G, ncmp, LANES), lambda b, i: (b, 0, 0, 0))
    kvs = lambda w: pl.BlockSpec((None, G, S, w * LANES), lambda b, i: (b, 0, 0, 0))
    return pl.pallas_call(
        functools.partial(_nsa_kernel, seq=S),
        grid=(B, S // TQ),
        in_specs=[
            pl.BlockSpec((None, TQ, width), lambda b, i: (b, i, 0)),
            kvc, kvc, kvs(2), kvs(2), kvs(1), kvs(2),
            pl.BlockSpec((None, TQ, LANES), lambda b, i: (b, i, CB_NG)),
            pl.BlockSpec((None, TQ, width), lambda b, i: (b, i, CB_NZ * LANES // width)),
        ],
        out_specs=pl.BlockSpec((None, TQ, width), lambda b, i: (b, i, 0)),
        out_shape=jax.ShapeDtypeStruct((B, S, width), BF16),
        scratch_shapes=[pltpu.VMEM((G, S // SLC_TK, NSA_GROUP * TQ, SLC_TK), F32)],
        compiler_params=pltpu.CompilerParams(
            dimension_semantics=("parallel", "arbitrary"), vmem_limit_bytes=VMEM_LIMIT),
        name="nsa_core",
    )(qn, kc, vc, ks, vs, kw, vw, proj_b, proj_a)


def _merge_kernel(a_ref, b_ref, ma_ref, mb_ref, x_ref, gate_ref, wa_ref, wb_ref, wo_ref, o_ref):
    ya = _dot(a_ref[...], wa_ref[...])
    yb = _dot(b_ref[...], wb_ref[...])
    y = _sigmoid(ma_ref[...].astype(F32)) * ya + _sigmoid(mb_ref[...].astype(F32)) * yb
    o_ref[...] = x_ref[...] + gate_ref[...] * _dot(y.astype(BF16), wo_ref[...])


def _merge(a2, b2, proj_a2, x2, mod_l, wa, wb, wo, seq):
    M, D = x2.shape
    tm = 256
    per_b = seq // tm
    B = mod_l.shape[1]
    mod4 = mod_l.reshape(3, B, 1, D)
    dcols = D // LANES
    full = lambda a: pl.BlockSpec(a.shape, lambda i: (0, 0))
    return pl.pallas_call(
        _merge_kernel,
        grid=(M // tm,),
        in_specs=[
            pl.BlockSpec((tm, a2.shape[1]), lambda i: (i, 0)),
            pl.BlockSpec((tm, b2.shape[1]), lambda i: (i, 0)),
            pl.BlockSpec((tm, D), lambda i: (i, CB_MA // dcols)),
            pl.BlockSpec((tm, D), lambda i: (i, CB_MB // dcols)),
            pl.BlockSpec((tm, D), lambda i: (i, 0)),
            pl.BlockSpec((None, None, 1, D), lambda i: (2, i // per_b, 0, 0)),
            full(wa), full(wb), full(wo),
        ],
        out_specs=pl.BlockSpec((tm, D), lambda i: (i, 0)),
        out_shape=jax.ShapeDtypeStruct((M, D), F32),
        compiler_params=pltpu.CompilerParams(
            dimension_semantics=("parallel",), vmem_limit_bytes=VMEM_LIMIT),
        name="merge_out",
    )(a2, b2, proj_a2, proj_a2, x2, mod4, wa, wb, wo)


def _permute_w_in(w_in):
    sb, nw, kv = SB_HEADS * HEAD_DIM, NSA_HEADS * HEAD_DIM, NSA_KV_HEADS * HEAD_DIM
    D = w_in.shape[1]
    sizes = (sb, sb, sb, sb, nw, kv, kv, kv, kv, kv, kv, nw, 3 * NSA_HEADS, D, D)
    offs = np.concatenate([[0], np.cumsum(sizes)])
    assert offs[-1] == w_in.shape[2]
    w_b = w_in.astype(BF16)
    seg = lambda k: w_b[:, :, offs[k]:offs[k + 1]]
    pad = lambda n: jnp.zeros(w_in.shape[:2] + (n,), BF16)
    sb_q = (w_in[:, :, offs[0]:offs[1]] * Q_SCALE2).astype(BF16)
    part_a = jnp.concatenate([sb_q, seg(1), seg(2), seg(3), seg(13), seg(14), seg(11),
                              seg(6), seg(8), seg(10), pad(LANES)], axis=-1)
    part_b = jnp.concatenate([seg(4), seg(5), seg(7), seg(9), seg(12), pad(LANES - 3 * NSA_HEADS)], axis=-1)
    assert part_a.shape[-1] == N_PROJ_A and part_b.shape[-1] == N_PROJ_B
    return jnp.concatenate([part_a, part_b], axis=-1)


def _rope_tables(pos):
    half = HEAD_DIM // 2
    freq = ROPE_THETA ** (-jnp.arange(half, dtype=F32) / half)
    ang = pos.astype(F32)[:, None] * freq[None, :]
    cos, sin = jnp.cos(ang), jnp.sin(ang)
    reps = LANES // HEAD_DIM
    return jnp.tile(jnp.concatenate([cos, cos], -1), (1, reps)), jnp.tile(jnp.concatenate([-sin, sin], -1), (1, reps))


def kernel(x, c, ada_w, ada_b, norm_g, w_in, q_norm_g, kc_norm_g, ks_norm_g, kw_norm_g, cmp_pe_k, cmp_w1_k, cmp_w2_k, cmp_pe_v, cmp_w1_v, cmp_w2_v, w_proj_a, w_proj_b, w_out):
    B, S, D = x.shape
    L = ada_w.shape[0]
    M = B * S
    assert D == 8 * LANES and S % 1024 == 0 and S // SLC_BLOCK >= SLC_TOPK and S >= WINDOW + TQ
    assert TQ == LANES and S // SLC_BLOCK <= LANES and S % SLC_TK == 0 and S % SB_T == 0
    n_cmp_rows = S // CMP_STRIDE

    mod = _modulation(c, ada_w, ada_b)
    w_p = _permute_w_in(w_in)
    wa, wb, wo = w_proj_a.astype(BF16), w_proj_b.astype(BF16), w_out.astype(BF16)
    cos, sin = _rope_tables(jnp.arange(S))
    cos_c, sin_c = _rope_tables(CMP_STRIDE * jnp.arange(n_cmp_rows) + CMP_LEN - 1)
    tile2 = lambda g: jnp.tile(g, (1, LANES // HEAD_DIM)).reshape(L, 1, LANES)
    q_g, kc_g, ks_g, kw_g = tile2(q_norm_g), tile2(kc_norm_g), tile2(ks_norm_g), tile2(kw_norm_g)

    x2 = x.reshape(M, D)
    for l in range(L):
        proj_a2, proj_b2 = _in_projection(x2, mod[l], norm_g[l], w_p[l], S)
        proj_a = proj_a2.reshape(B, S, N_PROJ_A)
        proj_b = proj_b2.reshape(B, S, N_PROJ_B)
        a = _sb_attention(proj_a)
        qn, ks, vs, kw, vw = _nsa_prep(proj_a, proj_b, cos, sin, q_g[l], ks_g[l], kw_g[l])
        xk = proj_b[:, :, CB_KC * LANES:(CB_KC + 1) * LANES].reshape(B, n_cmp_rows, CMP_STRIDE * LANES)
        xv = proj_a[:, :, CB_VC * LANES:(CB_VC + 1) * LANES].reshape(B, n_cmp_rows, CMP_STRIDE * LANES)
        kc, vc = _compress(xk, xv,
                           _expand_cmp_weights(cmp_pe_k[l], cmp_w1_k[l], cmp_w2_k[l]),
                           _expand_cmp_weights(cmp_pe_v[l], cmp_w1_v[l], cmp_w2_v[l]),
                           cos_c, sin_c, kc_g[l])
        b = _nsa_core(qn, kc, vc, ks, vs, kw, vw, proj_a, proj_b)
        x2 = _merge(a.reshape(M, -1), b.reshape(M, -1), proj_a2, x2, mod[l], wa[l], wb[l], wo[l], S)
    return x2.reshape(B, S, D)
```
